```python
import jax
import jax.numpy as jnp
from jax import lax
import numpy as np

D_MODEL = 1024
BATCH = 32
SEQ = 2048
DEPTH = 4
DEC_BATCH = 8
DEC_SEQ = 32
PAST_LEN = 2048

CHUNK = 64
N_MIXERS = 4
N_A = len(range(0, DEPTH, N_MIXERS))
N_B = len(range(1, DEPTH, N_MIXERS))
N_C = len(range(2, DEPTH, N_MIXERS))
N_D = len(range(3, DEPTH, N_MIXERS))
NORM_EPS = 1e-6

HG_DK = 128
HG_HEADS = D_MODEL // HG_DK
HG_DV = D_MODEL // HG_HEADS
HG_HK = HG_HEADS * HG_DK
HG_BLOCK = 16

RET_HEADS = 4
RET_DK = D_MODEL // RET_HEADS
RET_DV = 2 * RET_DK
RET_V = RET_HEADS * RET_DV
ROPE_BASE = 10000.0

RW_HEAD = 64
RW_HEADS = D_MODEL // RW_HEAD
RW_DECAY_LORA = 64
RW_A_LORA = 64
RW_G_LORA = 128
RW_GN_EPS = 64e-5

D_RNN = D_MODEL
LRU_BLOCKS = 8
LRU_BW = D_RNN // LRU_BLOCKS
CONV_W = 4
LRU_C = 8.0

D_FF = ((8 * D_MODEL // 3 + 255) // 256) * 256

kernel_name = 'hybrid_streaming_encoder_step'

F32 = jnp.float32


def rmsnorm(x, w):
    xf = x.astype(F32)
    y = xf * lax.rsqrt(jnp.mean(xf * xf, axis=-1, keepdims=True) + NORM_EPS)
    return (y * w.astype(F32)).astype(x.dtype)


def head_rmsnorm(o, w):
    B, T, H, V = o.shape
    o = o * lax.rsqrt(jnp.mean(o * o, axis=-1, keepdims=True) + NORM_EPS)
    return o.reshape(B, T, H * V) * w.astype(F32)


def chunked_gla(q, k, v, log_f, s0, block):
    B, T, H, _ = q.shape
    pad = (-T) % block
    def to_blocks(a):
        a = jnp.pad(a, ((0, 0), (0, pad), (0, 0), (0, 0)))
        return jnp.moveaxis(a.reshape(B, (T + pad) // block, block, H, a.shape[-1]), 1, 0)
    qb, kb, vb, fb = (to_blocks(a) for a in (q, k, v, log_f))
    cum = jnp.cumsum(fb, axis=2)
    cum_last = cum[:, :, -1:]
    q_in = qb * jnp.exp(cum)
    k_in = kb * jnp.exp(-cum)
    k_st = kb * jnp.exp(cum_last - cum)
    causal = jnp.tril(jnp.ones((block, block), dtype=bool))
    scores = jnp.where(causal, jnp.einsum('nbthk,nbshk->nbhts', q_in, k_in), 0.0)
    o_intra = jnp.einsum('nbhts,nbshv->nbthv', scores, vb)
    def step(S, xs):
        q_t, k_t, v_t, dec = xs
        o = jnp.einsum('bthk,bhkv->bthv', q_t, S)
        S = S * dec[:, 0][..., None] + jnp.einsum('bshk,bshv->bhkv', k_t, v_t)
        return S, o
    s_final, o_inter = lax.scan(step, s0, (q_in, k_st, vb, jnp.exp(cum_last)))
    o = jnp.moveaxis(o_intra + o_inter, 0, 1).reshape(B, T + pad, H, vb.shape[-1])[:, :T]
    return o, s_final


def rotary(x, pos):
    half = x.shape[-1] // 2
    inv = ROPE_BASE ** (-jnp.arange(half, dtype=F32) / half)
    ang = pos.astype(F32)[:, None] * inv[None, :]
    cos = jnp.cos(ang)[None, :, None, :]
    sin = jnp.sin(ang)[None, :, None, :]
    x1, x2 = x[..., :half], x[..., half:]
    return jnp.concatenate([x1 * cos - x2 * sin, x1 * sin + x2 * cos], axis=-1)


def hgrn2_mixer(h, s0, lb, w_in, g_norm, w_out):
    B, T, _ = h.shape
    proj = (h @ w_in).astype(F32)
    q, fl, v, g = jnp.split(proj, [HG_HK, 2 * HG_HK, 2 * HG_HK + D_MODEL], axis=-1)
    lb = lb.astype(F32)
    f = lb + (1.0 - lb) * jax.nn.sigmoid(fl)
    k = (1.0 - lb) * jax.nn.sigmoid(-fl)
    log_f = jnp.log(f)
    hd = lambda a, d: a.reshape(B, T, HG_HEADS, d)
    o, s = chunked_gla(hd(q, HG_DK), hd(k, HG_DK), hd(v, HG_DV), hd(log_f, HG_DK),
                       s0.astype(F32), HG_BLOCK)
    o = head_rmsnorm(o, g_norm) * jax.nn.silu(g)
    return o.astype(h.dtype) @ w_out, s.astype(s0.dtype)


def retention_mixer(h, s0, pos0, w_in, g_norm, w_out):
    B, T, _ = h.shape
    proj = (h @ w_in).astype(F32)
    q, k, v, g = jnp.split(proj, [D_MODEL, 2 * D_MODEL, 2 * D_MODEL + RET_V], axis=-1)
    pos = pos0 + jnp.arange(T)
    q = rotary(q.reshape(B, T, RET_HEADS, RET_DK), pos)
    k = rotary(k.reshape(B, T, RET_HEADS, RET_DK), pos) * (RET_DK ** -0.5)
    v = v.reshape(B, T, RET_HEADS, RET_DV)
    log_gamma = jnp.log1p(-jnp.exp2(-5.0 - jnp.arange(RET_HEADS, dtype=F32)))
    log_f = jnp.broadcast_to(log_gamma[None, None, :, None], (B, T, RET_HEADS, 1))
    o, s = chunked_gla(q, k, v, log_f, s0.astype(F32), CHUNK)
    o = head_rmsnorm(o, g_norm) * jax.nn.silu(g)
    return o.astype(h.dtype) @ w_out, s.astype(s0.dtype)


def rwkv7_mixer(h, shift0, s0, mu, w_rkv, w0, w1, w2, a0, a1, a2, g1, g2,
                k_k, k_a, r_k, ln_w, ln_b, w_out):
    B, T, D = h.shape
    h_prev = jnp.concatenate([shift0[:, None].astype(h.dtype), h[:, :-1]], axis=1)
    xm = h[None] + (h_prev - h)[None] * mu[:, None, None, :]
    rkv = jnp.einsum('nbtd,nde->nbte', xm[:3], w_rkv).astype(F32)
    r, k, v = rkv[0], rkv[1], rkv[2]
    xw, xa, xg = xm[3], xm[4], xm[5]
    w_log = -jax.nn.softplus(-(w0 + jnp.tanh(xw @ w1) @ w2).astype(F32)) - 0.5
    decay = jnp.exp(-jnp.exp(w_log))
    a = jax.nn.sigmoid((a0 + (xa @ a1) @ a2).astype(F32))
    g = (jax.nn.sigmoid(xg @ g1) @ g2).astype(F32)
    hd = lambda t: t.reshape(B, T, RW_HEADS, RW_HEAD)
    kk = hd(k * k_k.astype(F32))
    kk = kk / jnp.maximum(jnp.linalg.norm(kk, axis=-1, keepdims=True), 1e-12)
    k = k * (1.0 + (a - 1.0) * k_a.astype(F32))
    r, k, v, a, decay = hd(r), hd(k), hd(v), hd(a), hd(decay)
    def step(S, xs):
        r_t, k_t, v_t, kk_t, a_t, w_t = xs
        sa = jnp.einsum('bhvk,bhk->bhv', S, -kk_t)
        S = (S * w_t[:, :, None, :] + sa[..., None] * (kk_t * a_t)[:, :, None, :]
             + v_t[..., None] * k_t[:, :, None, :])
        return S, jnp.einsum('bhvk,bhk->bhv', S, r_t)
    tm = lambda t: jnp.moveaxis(t, 1, 0)
    s, y = lax.scan(step, s0.astype(F32), (tm(r), tm(k), tm(v), tm(kk), tm(a), tm(decay)))
    y = jnp.moveaxis(y, 0, 1)
    y_mu = jnp.mean(y, axis=-1, keepdims=True)
    y_var = jnp.mean(jnp.square(y - y_mu), axis=-1, keepdims=True)
    yn = ((y - y_mu) * lax.rsqrt(y_var + RW_GN_EPS)).reshape(B, T, D) * ln_w.astype(F32) + ln_b.astype(F32)
    bonus = (jnp.sum(r * k * r_k.astype(F32), axis=-1, keepdims=True) * v).reshape(B, T, D)
    out = ((yn + bonus) * g).astype(h.dtype) @ w_out
    return out, s.astype(s0.dtype), h[:, -1].astype(shift0.dtype)


def rglru_mixer(h, h0, conv0, w_in, conv_w, conv_b, w_gates, b_gates, lam, w_out):
    B, T, _ = h.shape
    gate_in, xb = jnp.split(h @ w_in, 2, axis=-1)
    gate = jax.nn.gelu(gate_in.astype(F32), approximate=True)
    xpad = jnp.concatenate([conv0.astype(xb.dtype), xb], axis=1)
    xc = conv_b.astype(F32)
    for j in range(CONV_W):
        xc = xc + xpad[:, j:j + T].astype(F32) * conv_w[j].astype(F32)
    xbk = xc.reshape(B, T, LRU_BLOCKS, LRU_BW)
    gates = jnp.einsum('btnc,gncd->gbtnd', xbk, w_gates.astype(F32)).reshape(2, B, T, D_RNN)
    gates = gates + b_gates.astype(F32)[:, None, None, :]
    r_gate = jax.nn.sigmoid(gates[0])
    i_gate = jax.nn.sigmoid(gates[1])
    log_a = -LRU_C * r_gate * jax.nn.softplus(-lam.astype(F32))
    a = jnp.exp(log_a)
    b = jnp.sqrt(-jnp.expm1(2.0 * log_a)) * (i_gate * xc)
    b = b.at[:, 0].add(a[:, 0] * h0.astype(F32))
    def combine(left, right):
        a1, b1 = left
        a2, b2 = right
        return a1 * a2, a2 * b1 + b2
    _, hs = lax.associative_scan(combine, (a, b), axis=1)
    y = (hs * gate).astype(h.dtype) @ w_out
    return y, hs[:, -1].astype(h0.dtype), xpad[:, -(CONV_W - 1):].astype(conv0.dtype)


def swiglu(h, w_in, w_out):
    gt, up = jnp.split(h @ w_in, 2, axis=-1)
    return (jax.nn.silu(gt) * up) @ w_out


def trunk(x, pos0, st_hg, st_ret, st_rw, st_sh, st_lru, st_conv, p):
    lb_all = jnp.cumsum(jax.nn.softmax(p['hg_lb'].astype(F32), axis=0), axis=0)
    new_hg, new_ret, new_rw, new_sh, new_lru, new_conv = [], [], [], [], [], []
    for i in range(DEPTH):
        m, j = i % N_MIXERS, i // N_MIXERS
        h = rmsnorm(x, p['norm_mix'][i])
        if m == 0:
            y, s = hgrn2_mixer(h, st_hg[j], lb_all[i], p['hg_w_in'][j], p['hg_norm'][j], p['hg_w_out'][j])
            new_hg.append(s)
        elif m == 1:
            y, s = retention_mixer(h, st_ret[j], pos0, p['ret_w_in'][j], p['ret_norm'][j], p['ret_w_out'][j])
            new_ret.append(s)
        elif m == 2:
            y, s, sh = rwkv7_mixer(h, st_sh[j], st_rw[j], p['rw_mu'][j], p['rw_w_rkv'][j], p['rw_w0'][j],
                                   p['rw_w1'][j], p['rw_w2'][j], p['rw_a0'][j], p['rw_a1'][j], p['rw_a2'][j],
                                   p['rw_g1'][j], p['rw_g2'][j], p['rw_k_k'][j], p['rw_k_a'][j], p['rw_r_k'][j],
                                   p['rw_ln_w'][j], p['rw_ln_b'][j], p['rw_w_out'][j])
            new_rw.append(s)
            new_sh.append(sh)
        else:
            y, s, c = rglru_mixer(h, st_lru[j], st_conv[j], p['lru_w_in'][j], p['lru_conv_w'][j],
                                  p['lru_conv_b'][j], p['lru_w_gates'][j], p['lru_b_gates'][j],
                                  p['lru_lambda'][j], p['lru_w_out'][j])
            new_lru.append(s)
            new_conv.append(c)
        x = x + y
        x = x + swiglu(rmsnorm(x, p['norm_ffn'][i]), p['ffn_w_in'][i], p['ffn_w_out'][i])
    out = rmsnorm(x, p['norm_final'])
    return (out, jnp.stack(new_hg), jnp.stack(new_ret), jnp.stack(new_rw), jnp.stack(new_sh),
            jnp.stack(new_lru), jnp.stack(new_conv))


def setup_inputs(seed: int = 0) -> dict:
    key = jax.random.key(seed)
    ks = iter(jax.random.split(key, 64))
    def nrm(shape, scale):
        return jax.random.normal(next(ks), shape, F32) * scale
    D = D_MODEL
    u = jax.random.uniform(next(ks), (N_D, D_RNN), F32, minval=0.9, maxval=0.999)
    s_lam = u ** (1.0 / LRU_C)
    return {
        'x_prompt': nrm((BATCH, SEQ, D), 1.0),
        'x_sample': nrm((DEC_BATCH, DEC_SEQ, D), 1.0),
        'state_hgrn': nrm((N_A, DEC_BATCH, HG_HEADS, HG_DK, HG_DV), 0.5),
        'state_ret': nrm((N_B, DEC_BATCH, RET_HEADS, RET_DK, RET_DV), 1.0),
        'state_rwkv': nrm((N_C, DEC_BATCH, RW_HEADS, RW_HEAD, RW_HEAD), 0.5),
        'state_rwkv_shift': nrm((N_C, DEC_BATCH, D), 1.0),
        'state_lru': nrm((N_D, DEC_BATCH, D_RNN), 0.5),
        'state_lru_conv': nrm((N_D, DEC_BATCH, CONV_W - 1, D_RNN), 1.0),
        'norm_mix': 1.0 + nrm((DEPTH, D), 0.02),
        'norm_ffn': 1.0 + nrm((DEPTH, D), 0.02),
        'norm_final': 1.0 + nrm((D,), 0.02),
        'hg_lb': nrm((DEPTH + 1, HG_HK), 0.1),
        'hg_w_in': nrm((N_A, D, 2 * HG_HK + 2 * D), D ** -0.5),
        'hg_norm': 1.0 + nrm((N_A, D), 0.02),
        'hg_w_out': nrm((N_A, D, D), D ** -0.5),
        'ret_w_in': nrm((N_B, D, 2 * D + 2 * RET_V), D ** -0.5),
        'ret_norm': 1.0 + nrm((N_B, RET_V), 0.02),
        'ret_w_out': nrm((N_B, RET_V, D), RET_V ** -0.5),
        'rw_mu': jax.random.uniform(next(ks), (N_C, 6, D), F32),
        'rw_w_rkv': nrm((N_C, 3, D, D), D ** -0.5),
        'rw_w0': nrm((N_C, D), 1.0),
        'rw_w1': nrm((N_C, D, RW_DECAY_LORA), D ** -0.5),
        'rw_w2': nrm((N_C, RW_DECAY_LORA, D), 0.1 * RW_DECAY_LORA ** -0.5),
        'rw_a0': nrm((N_C, D), 0.5),
        'rw_a1': nrm((N_C, D, RW_A_LORA), D ** -0.5),
        'rw_a2': nrm((N_C, RW_A_LORA, D), 0.1 * RW_A_LORA ** -0.5),
        'rw_g1': nrm((N_C, D, RW_G_LORA), D ** -0.5),
        'rw_g2': nrm((N_C, RW_G_LORA, D), RW_G_LORA ** -0.5),
        'rw_k_k': 0.85 + nrm((N_C, D), 0.05),
        'rw_k_a': 1.0 + nrm((N_C, D), 0.05),
        'rw_r_k': nrm((N_C, RW_HEADS, RW_HEAD), 0.1),
        'rw_ln_w': 1.0 + nrm((N_C, D), 0.02),
        'rw_ln_b': nrm((N_C, D), 0.01),
        'rw_w_out': nrm((N_C, D, D), D ** -0.5),
        'lru_w_in': nrm((N_D, D, 2 * D_RNN), D ** -0.5),
        'lru_conv_w': nrm((N_D, CONV_W, D_RNN), CONV_W ** -0.5),
        'lru_conv_b': nrm((N_D, D_RNN), 0.01),
        'lru_w_gates': nrm((N_D, 2, LRU_BLOCKS, LRU_BW, LRU_BW), LRU_BW ** -0.5),
        'lru_b_gates': nrm((N_D, 2, D_RNN), 0.01),
        'lru_lambda': jnp.log(s_lam) - jnp.log1p(-s_lam),
        'lru_w_out': nrm((N_D, D_RNN, D), D_RNN ** -0.5),
        'ffn_w_in': nrm((DEPTH, D, 2 * D_FF), D ** -0.5),
        'ffn_w_out': nrm((DEPTH, D_FF, D), D_FF ** -0.5),
    }


def reference(x_prompt, x_sample, state_hgrn, state_ret, state_rwkv, state_rwkv_shift, state_lru,
              state_lru_conv, norm_mix, norm_ffn, norm_final, hg_lb, hg_w_in, hg_norm, hg_w_out,
              ret_w_in, ret_norm, ret_w_out, rw_mu, rw_w_rkv, rw_w0, rw_w1, rw_w2, rw_a0, rw_a1,
              rw_a2, rw_g1, rw_g2, rw_k_k, rw_k_a, rw_r_k, rw_ln_w, rw_ln_b, rw_w_out, lru_w_in,
              lru_conv_w, lru_conv_b, lru_w_gates, lru_b_gates, lru_lambda, lru_w_out,
              ffn_w_in, ffn_w_out):
    p = dict(norm_mix=norm_mix, norm_ffn=norm_ffn, norm_final=norm_final, hg_lb=hg_lb,
             hg_w_in=hg_w_in, hg_norm=hg_norm, hg_w_out=hg_w_out, ret_w_in=ret_w_in,
             ret_norm=ret_norm, ret_w_out=ret_w_out, rw_mu=rw_mu, rw_w_rkv=rw_w_rkv, rw_w0=rw_w0,
             rw_w1=rw_w1, rw_w2=rw_w2, rw_a0=rw_a0, rw_a1=rw_a1, rw_a2=rw_a2, rw_g1=rw_g1,
             rw_g2=rw_g2, rw_k_k=rw_k_k, rw_k_a=rw_k_a, rw_r_k=rw_r_k, rw_ln_w=rw_ln_w,
             rw_ln_b=rw_ln_b, rw_w_out=rw_w_out, lru_w_in=lru_w_in, lru_conv_w=lru_conv_w,
             lru_conv_b=lru_conv_b, lru_w_gates=lru_w_gates, lru_b_gates=lru_b_gates,
             lru_lambda=lru_lambda, lru_w_out=lru_w_out, ffn_w_in=ffn_w_in, ffn_w_out=ffn_w_out)
    bp, dt = x_prompt.shape[0], x_prompt.dtype
    y_prompt, hgrn_p, ret_p, rwkv_p, shift_p, lru_p, conv_p = trunk(
        x_prompt, 0,
        jnp.zeros((N_A, bp, HG_HEADS, HG_DK, HG_DV), dt),
        jnp.zeros((N_B, bp, RET_HEADS, RET_DK, RET_DV), dt),
        jnp.zeros((N_C, bp, RW_HEADS, RW_HEAD, RW_HEAD), dt),
        jnp.zeros((N_C, bp, D_MODEL), dt),
        jnp.zeros((N_D, bp, D_RNN), dt),
        jnp.zeros((N_D, bp, CONV_W - 1, D_RNN), dt),
        p)
    y_sample, hgrn_s, ret_s, rwkv_s, shift_s, lru_s, conv_s = trunk(
        x_sample, PAST_LEN, state_hgrn, state_ret, state_rwkv, state_rwkv_shift, state_lru,
        state_lru_conv, p)
    return (y_prompt, y_sample, hgrn_p, hgrn_s, ret_p, ret_s, rwkv_p, rwkv_s,
            shift_p, shift_s, lru_p, lru_s, conv_p, conv_s)
```

```python
import functools
import math

import jax
import jax.numpy as jnp
from jax import lax
from jax.experimental import pallas as pl
from jax.experimental.pallas import tpu as pltpu

F32 = jnp.float32
BF16 = jnp.bfloat16

D_MODEL = 1024
NORM_EPS = 1e-6
HG_HEADS, HG_DK, HG_BLOCK = 8, 128, 16
RET_HEADS, RET_DK, RET_DV = 4, 256, 512
ROPE_BASE = 10000.0
RW_HEADS, RW_HEAD = 16, 64
RW_GROUP = 4
RW_GN_EPS = 64e-5
LRU_BLOCKS, LRU_BW, CONV_W, LRU_C = 8, 128, 4, 8.0
D_FF = 2816
PAST_LEN = 2048

VMEM_LIMIT_BYTES = 56 * 1024 * 1024


def _cparams(*sem):
    return pltpu.CompilerParams(dimension_semantics=sem, vmem_limit_bytes=VMEM_LIMIT_BYTES)


def _const_spec(shape):
    n = len(shape)
    return pl.BlockSpec(tuple(shape), lambda *_: (0,) * n, pipeline_mode=pl.Buffered(1))


def _iota(shape, dim):
    return lax.broadcasted_iota(jnp.int32, shape, dim)


def _dot(a, b):
    return jnp.dot(a, b, preferred_element_type=F32)


def _dot_nt(a, b):
    return lax.dot_general(a, b, (((1,), (1,)), ((), ())), preferred_element_type=F32)


def _dot_tn(a, b):
    return lax.dot_general(a, b, (((0,), (0,)), ((), ())), preferred_element_type=F32)


def _split3(x):
    hi = x.astype(BF16)
    r1 = x - hi.astype(F32)
    mid = r1.astype(BF16)
    lo = (r1 - mid.astype(F32)).astype(BF16)
    return hi, mid, lo


def _sel_dot(m01, x):
    hi, mid, lo = _split3(x)
    return _dot(m01, hi) + _dot(m01, mid) + _dot(m01, lo)


def _dot_sel(x, m01):
    hi, mid, lo = _split3(x)
    return _dot(hi, m01) + _dot(mid, m01) + _dot(lo, m01)


def _rms(x, w):
    return x * lax.rsqrt(jnp.mean(x * x, axis=-1, keepdims=True) + NORM_EPS) * w


def _softplus(x):
    return jnp.maximum(x, 0.0) + jnp.log1p(jnp.exp(-jnp.abs(x)))


def _log2(n):
    l = int(math.log2(n))
    assert 1 << l == n, n
    return l


def _norm_mm_body(x_ref, nw_ref, w_ref, o_ref):
    h = _rms(x_ref[...], nw_ref[...])
    o_ref[...] = _dot(h.astype(BF16), w_ref[...])


def norm_matmul(x2d, nw, w, *, tm=512, tn=2048):
    m, k = x2d.shape
    n = w.shape[1]
    tm, tn = min(tm, m), min(tn, n)
    return pl.pallas_call(
        _norm_mm_body,
        grid=(m // tm, n // tn),
        in_specs=[pl.BlockSpec((tm, k), lambda i, j: (i, 0)),
                  pl.BlockSpec((1, k), lambda i, j: (0, 0)),
                  pl.BlockSpec((k, tn), lambda i, j: (0, j))],
        out_specs=pl.BlockSpec((tm, tn), lambda i, j: (i, j)),
        out_shape=jax.ShapeDtypeStruct((m, n), F32),
        compiler_params=_cparams("parallel", "arbitrary"),
        name="norm_matmul",
    )(x2d, nw.reshape(1, k), w)


def _mix_ffn_body(o_ref, wo_ref, x_ref, nw_ref, win_ref, wout_ref, fnw_ref, y_ref, *, final_norm):
    x1 = x_ref[...] + _dot(o_ref[...].astype(BF16), wo_ref[...])
    h = _rms(x1, nw_ref[...]).astype(BF16)
    half = D_FF // 2
    acc = x1
    for c in range(2):
        g = _dot(h, win_ref[:, c * half:(c + 1) * half])
        u = _dot(h, win_ref[:, D_FF + c * half:D_FF + (c + 1) * half])
        act = (g * jax.nn.sigmoid(g) * u).astype(BF16)
        acc = acc + _dot(act, wout_ref[c * half:(c + 1) * half, :])
    if final_norm:
        acc = _rms(acc, fnw_ref[...])
    y_ref[...] = acc


def mix_out_ffn(o2d, w_o, x2d, nw, w_in, w_out, fnw, *, final_norm, tm=256):
    m, ko = o2d.shape
    d = x2d.shape[1]
    tm = min(tm, m)
    return pl.pallas_call(
        functools.partial(_mix_ffn_body, final_norm=final_norm),
        grid=(m // tm,),
        in_specs=[pl.BlockSpec((tm, ko), lambda i: (i, 0)),
                  _const_spec((ko, d)),
                  pl.BlockSpec((tm, d), lambda i: (i, 0)),
                  _const_spec((1, d)),
                  _const_spec((d, 2 * D_FF)),
                  _const_spec((D_FF, d)),
                  _const_spec((1, d))],
        out_specs=pl.BlockSpec((tm, d), lambda i: (i, 0)),
        out_shape=jax.ShapeDtypeStruct((m, d), F32),
        compiler_params=_cparams("parallel"),
        name="mix_out_ffn",
    )(o2d, w_o, x2d, nw.reshape(1, d), w_in, w_out, fnw.reshape(1, d))


def _hgrn_body(*refs, tc, layer, has_state):
    if has_state:
        (q_ref, f_ref, v_ref, g_ref, lbp_ref, gn_ref, s0_ref, o_ref, sout_ref,
         st_ref, qin_ref, kst_ref, dec_ref, oacc_ref) = refs
    else:
        (q_ref, f_ref, v_ref, g_ref, lbp_ref, gn_ref, o_ref, sout_ref,
         st_ref, qin_ref, kst_ref, dec_ref, oacc_ref) = refs
    t = pl.program_id(1)
    blk = min(HG_BLOCK, tc)
    heads, dk = HG_HEADS, HG_DK

    @pl.when(t == 0)
    def _():
        for h in range(heads):
            st_ref[h] = s0_ref[0, h].T if has_state else jnp.zeros((dk, dk), F32)

    lbp = lbp_ref[...]
    e = jnp.exp(lbp - jnp.max(lbp, axis=0, keepdims=True))
    lb = jnp.sum(e[:layer + 1], axis=0, keepdims=True) / jnp.sum(e, axis=0, keepdims=True)

    fl = f_ref[0]
    f = lb + (1.0 - lb) * jax.nn.sigmoid(fl)
    k = (1.0 - lb) * jax.nn.sigmoid(-fl)
    lf = jnp.log(f)
    sh = _log2(blk)
    ri, ci = _iota((tc, tc), 0), _iota((tc, tc), 1)
    same = (ri >> sh) == (ci >> sh)
    causal = same & (ci <= ri)
    cum = _sel_dot(causal.astype(BF16), lf)
    tot = _sel_dot(same.astype(BF16), lf)
    q_in = q_ref[0] * jnp.exp(cum)
    k_in = k * jnp.exp(-cum)
    qin_ref[...] = q_in
    kst_ref[...] = k * jnp.exp(tot - cum)
    dec_ref[...] = jnp.exp(tot)
    v = v_ref[0]
    for h in range(heads):
        sl = slice(h * dk, (h + 1) * dk)
        s = _dot_nt(q_in[:, sl].astype(BF16), k_in[:, sl].astype(BF16))
        s = jnp.where(causal, s, 0.0)
        oacc_ref[:, sl] = _dot(s.astype(BF16), v[:, sl].astype(BF16))

    def sub(i, carry):
        r0 = pl.multiple_of(i * blk, blk)
        rows = pl.ds(r0, blk)
        for h in range(heads):
            sl = slice(h * dk, (h + 1) * dk)
            st = st_ref[h]
            oacc_ref[rows, sl] += _dot_nt(qin_ref[rows, sl].astype(BF16), st.astype(BF16))
            kv = _dot_tn(v_ref[0, rows, sl].astype(BF16), kst_ref[rows, sl].astype(BF16))
            st_ref[h] = st * dec_ref[pl.ds(r0, 1), sl] + kv
        return carry

    lax.fori_loop(0, tc // blk, sub, 0)

    o = oacc_ref[...]
    g = g_ref[0]
    gn = gn_ref[...]
    for h in range(heads):
        sl = slice(h * dk, (h + 1) * dk)
        oh = o[:, sl]
        oh = oh * lax.rsqrt(jnp.mean(oh * oh, axis=-1, keepdims=True) + NORM_EPS)
        gh = g[:, sl]
        o_ref[0, :, sl] = oh * gn[:, sl] * (gh * jax.nn.sigmoid(gh))

    @pl.when(t == pl.num_programs(1) - 1)
    def _():
        for h in range(heads):
            sout_ref[0, h] = st_ref[h].T


def hgrn_scan(proj, hg_lb, g_norm, s0, *, layer, tc=128):
    b, t, _ = proj.shape
    d = D_MODEL
    tc = min(tc, t)
    has_state = s0 is not None
    col = lambda c: pl.BlockSpec((1, tc, d), lambda i, j, c=c: (i, j, c))
    in_specs = [col(0), col(1), col(2), col(3),
                pl.BlockSpec(hg_lb.shape, lambda i, j: (0, 0)),
                pl.BlockSpec((1, d), lambda i, j: (0, 0))]
    args = [proj, proj, proj, proj, hg_lb, g_norm.reshape(1, d)]
    st_spec = pl.BlockSpec((1, HG_HEADS, HG_DK, HG_DK), lambda i, j: (i, 0, 0, 0))
    if has_state:
        in_specs.append(st_spec)
        args.append(s0)
    return pl.pallas_call(
        functools.partial(_hgrn_body, tc=tc, layer=layer, has_state=has_state),
        grid=(b, t // tc),
        in_specs=in_specs,
        out_specs=[pl.BlockSpec((1, tc, d), lambda i, j: (i, j, 0)), st_spec],
        out_shape=[jax.ShapeDtypeStruct((b, t, d), F32),
                   jax.ShapeDtypeStruct((b, HG_HEADS, HG_DK, HG_DK), F32)],
        scratch_shapes=[pltpu.VMEM((HG_HEADS, HG_DK, HG_DK), F32)] + [pltpu.VMEM((tc, d), F32)] * 4,
        compiler_params=_cparams("parallel", "arbitrary"),
        name="hgrn_scan",
    )(*args)


def _rope_body(cos_ref, sin_ref, *, pos0, tc):
    half = RET_DK // 2
    pos = (pos0 + pl.program_id(0) * tc + _iota((tc, half), 0)).astype(F32)
    inv = jnp.power(jnp.float32(ROPE_BASE), -(_iota((tc, half), 1).astype(F32) / half))
    ang = pos * inv
    cos_ref[...] = jnp.cos(ang)
    sin_ref[...] = jnp.sin(ang)


def rope_table(pos0, t, *, tc=256):
    tc = min(tc, t)
    half = RET_DK // 2
    return pl.pallas_call(
        functools.partial(_rope_body, pos0=pos0, tc=tc),
        grid=(t // tc,),
        out_specs=[pl.BlockSpec((tc, half), lambda i: (i, 0))] * 2,
        out_shape=[jax.ShapeDtypeStruct((t, half), F32)] * 2,
        compiler_params=_cparams("parallel"),
        name="rope_table",
    )()


def _ret_body(*refs, tc, has_state):
    if has_state:
        q_ref, k_ref, v_ref, g_ref, cos_ref, sin_ref, gn_ref, s0_ref, o_ref, sout_ref, s_ref = refs
    else:
        q_ref, k_ref, v_ref, g_ref, cos_ref, sin_ref, gn_ref, o_ref, sout_ref, s_ref = refs
    head = pl.program_id(1)
    t = pl.program_id(2)
    half = RET_DK // 2

    @pl.when(t == 0)
    def _():
        s_ref[...] = s0_ref[0, 0] if has_state else jnp.zeros((RET_DK, RET_DV), F32)

    cos, sin = cos_ref[...], sin_ref[...]

    def rot(x):
        x1, x2 = x[:, :half], x[:, half:]
        return jnp.concatenate([x1 * cos - x2 * sin, x1 * sin + x2 * cos], axis=-1)

    q = rot(q_ref[0])
    k = rot(k_ref[0]) * (RET_DK ** -0.5)
    v = v_ref[0].astype(BF16)
    lg = jnp.log1p(-jnp.exp2(-5.0 - jnp.full((1, 1), head, jnp.int32).astype(F32)))
    ri, ci = _iota((tc, tc), 0), _iota((tc, tc), 1)
    decay = jnp.where(ri >= ci, jnp.exp((ri - ci).astype(F32) * lg), 0.0)
    row = _iota((tc, 1), 0).astype(F32)
    s = s_ref[...]
    o = _dot((_dot_nt(q.astype(BF16), k.astype(BF16)) * decay).astype(BF16), v)
    o = o + _dot((q * jnp.exp((row + 1.0) * lg)).astype(BF16), s.astype(BF16))
    k_st = k * jnp.exp((tc - 1.0 - row) * lg)
    s_new = s * jnp.exp(tc * lg) + _dot_tn(k_st.astype(BF16), v)
    s_ref[...] = s_new
    o = o * lax.rsqrt(jnp.mean(o * o, axis=-1, keepdims=True) + NORM_EPS)
    g = g_ref[0]
    o_ref[0] = o * gn_ref[...] * (g * jax.nn.sigmoid(g))

    @pl.when(t == pl.num_programs(2) - 1)
    def _():
        sout_ref[0, 0] = s_new


def ret_scan(proj, cos, sin, g_norm, s0, *, tc=256):
    b, t, _ = proj.shape
    tc = min(tc, t)
    has_state = s0 is not None
    nh, dk, dv = RET_HEADS, RET_DK, RET_DV
    half = dk // 2
    in_specs = [pl.BlockSpec((1, tc, dk), lambda i, h, j: (i, j, h)),
                pl.BlockSpec((1, tc, dk), lambda i, h, j: (i, j, nh + h)),
                pl.BlockSpec((1, tc, dv), lambda i, h, j: (i, j, nh + h)),
                pl.BlockSpec((1, tc, dv), lambda i, h, j: (i, j, 2 * nh + h)),
                pl.BlockSpec((tc, half), lambda i, h, j: (j, 0)),
                pl.BlockSpec((tc, half), lambda i, h, j: (j, 0)),
                pl.BlockSpec((1, dv), lambda i, h, j: (0, h))]
    args = [proj, proj, proj, proj, cos, sin, g_norm.reshape(1, nh * dv)]
    st_spec = pl.BlockSpec((1, 1, dk, dv), lambda i, h, j: (i, h, 0, 0))
    if has_state:
        in_specs.append(st_spec)
        args.append(s0)
    return pl.pallas_call(
        functools.partial(_ret_body, tc=tc, has_state=has_state),
        grid=(b, nh, t // tc),
        in_specs=in_specs,
        out_specs=[pl.BlockSpec((1, tc, dv), lambda i, h, j: (i, j, h)), st_spec],
        out_shape=[jax.ShapeDtypeStruct((b, t, nh * dv), F32),
                   jax.ShapeDtypeStruct((b, nh, dk, dv), F32)],
        scratch_shapes=[pltpu.VMEM((dk, dv), F32)],
        compiler_params=_cparams("parallel", "parallel", "arbitrary"),
        name="ret_scan",
    )(*args)


def _rwkv_proj_body(*refs, has_state):
    if has_state:
        (x_ref, nw_ref, sh0_ref, mu_ref, wrkv_ref, w0_ref, w1_ref, w2_ref, a0_ref, a1_ref, a2_ref,
         g1_ref, g2_ref, kk_ref, ka_ref, hsum_ref, hexp_ref,
         r_o, k_o, v_o, kk_o, a_o, lw_o, g_o, sh_o, carry_ref) = refs
    else:
        (x_ref, nw_ref, mu_ref, wrkv_ref, w0_ref, w1_ref, w2_ref, a0_ref, a1_ref, a2_ref,
         g1_ref, g2_ref, kk_ref, ka_ref, hsum_ref, hexp_ref,
         r_o, k_o, v_o, kk_o, a_o, lw_o, g_o, sh_o, carry_ref) = refs
    t = pl.program_id(1)
    tm = x_ref.shape[1]

    @pl.when(t == 0)
    def _():
        carry_ref[...] = sh0_ref[0] if has_state else jnp.zeros((1, D_MODEL), F32)

    h = _rms(x_ref[0], nw_ref[...])
    prev = jnp.where(_iota((tm, 1), 0) == 0, carry_ref[...], pltpu.roll(h, 1, axis=0))
    last = h[tm - 1:tm, :]
    carry_ref[...] = last
    sh_o[0] = last
    d = prev - h
    mix = lambda i: (h + d * mu_ref[i:i + 1, :]).astype(BF16)
    r = _dot(mix(0), wrkv_ref[0])
    k = _dot(mix(1), wrkv_ref[1])
    v = _dot(mix(2), wrkv_ref[2])
    w_pre = w0_ref[...] + _dot(jnp.tanh(_dot(mix(3), w1_ref[...])).astype(BF16), w2_ref[...])
    lw_o[0] = -jnp.exp(-_softplus(-w_pre) - 0.5)
    a = jax.nn.sigmoid(a0_ref[...] + _dot(_dot(mix(4), a1_ref[...]).astype(BF16), a2_ref[...]))
    g_o[0] = _dot(jax.nn.sigmoid(_dot(mix(5), g1_ref[...])).astype(BF16), g2_ref[...])
    kk = k * kk_ref[...]
    ss = _dot_sel(kk * kk, hsum_ref[...])
    inv = 1.0 / jnp.maximum(jnp.sqrt(ss), 1e-12)
    kk_o[0] = kk * _dot_sel(inv, hexp_ref[...])
    r_o[0] = r
    k_o[0] = k * (1.0 + (a - 1.0) * ka_ref[...])
    v_o[0] = v
    a_o[0] = a


def rwkv_proj(x, nw, shift0, p, *, tm=256):
    b, t, d = x.shape
    tm = min(tm, t)
    has_state = shift0 is not None
    row = lambda a: a.reshape(1, d)
    head_of_lane = jnp.arange(d) // RW_HEAD
    hsum = (head_of_lane[:, None] == jnp.arange(RW_HEADS)[None, :]).astype(BF16)
    hexp = hsum.T
    full = lambda a: pl.BlockSpec(a.shape, lambda i, j, n=a.ndim: (0,) * n)
    tok = pl.BlockSpec((1, tm, d), lambda i, j: (i, j, 0))
    args = [x, row(nw)]
    in_specs = [tok, full(row(nw))]
    if has_state:
        args.append(shift0.reshape(b, 1, d))
        in_specs.append(pl.BlockSpec((1, 1, d), lambda i, j: (i, 0, 0)))
    consts = [p['mu'], p['w_rkv'], row(p['w0']), p['w1'], p['w2'], row(p['a0']), p['a1'], p['a2'],
              p['g1'], p['g2'], row(p['k_k']), row(p['k_a']), hsum, hexp]
    args += consts
    in_specs += [full(a) for a in consts]
    outs = pl.pallas_call(
        functools.partial(_rwkv_proj_body, has_state=has_state),
        grid=(b, t // tm),
        in_specs=in_specs,
        out_specs=[tok] * 7 + [pl.BlockSpec((1, 1, d), lambda i, j: (i, 0, 0))],
        out_shape=[jax.ShapeDtypeStruct((b, t, d), F32)] * 7 + [jax.ShapeDtypeStruct((b, 1, d), F32)],
        scratch_shapes=[pltpu.VMEM((1, d), F32)],
        compiler_params=_cparams("parallel", "arbitrary"),
        name="rwkv_proj",
    )(*args)
    return outs[:7], outs[7].reshape(b, d)


def _rwkv_scan_body(*refs, L, has_state):
    if has_state:
        (r_ref, k_ref, v_ref, kk_ref, a_ref, lw_ref, g_ref, rk_ref, lnw_ref, lnb_ref, s0_ref,
         o_ref, sout_ref, s_ref) = refs
    else:
        (r_ref, k_ref, v_ref, kk_ref, a_ref, lw_ref, g_ref, rk_ref, lnw_ref, lnb_ref,
         o_ref, sout_ref, s_ref) = refs
    t = pl.program_id(1)
    gw = RW_GROUP * RW_HEAD
    ngroups = RW_HEADS // RW_GROUP
    sl_l, sl_h = _log2(L), _log2(RW_HEAD)
    n = RW_GROUP * L

    @pl.when(t == 0)
    def _():
        s_ref[...] = s0_ref[0] if has_state else jnp.zeros(s_ref.shape, F32)

    lw = lw_ref[0]
    tri = (_iota((L, L), 1) <= _iota((L, L), 0)).astype(BF16)
    cum = _sel_dot(tri, lw)
    cum_l = cum[L - 1:L, :]
    e_pos, e_neg = jnp.exp(cum), jnp.exp(-cum)
    e_tail = jnp.exp(cum_l - cum)
    kk, a, r, k, v = kk_ref[0], a_ref[0], r_ref[0], k_ref[0], v_ref[0]
    beta = kk * a
    a_hat = -kk * jnp.exp(cum - lw)
    r_hat = r * e_pos
    b_hat = beta * e_neg
    k_hat = k * e_neg
    b_til = beta * e_tail
    k_til = k * e_tail
    gam_l = jnp.exp(cum_l)
    bonus_in = r * k * rk_ref[...]

    own = (_iota((n, 1), 0) >> sl_l) == (_iota((1, gw), 1) >> sl_h)
    tile = lambda x: jnp.concatenate([x] * RW_GROUP, axis=0)
    stack = lambda x: jnp.where(own, tile(x), 0.0)
    unstack = lambda x: sum(x[i * L:(i + 1) * L] for i in range(RW_GROUP))
    ri, ci = _iota((n, n), 0), _iota((n, n), 1)
    same = (ri >> sl_l) == (ci >> sl_l)
    strict = same & (ri > ci)
    incl = same & (ri >= ci)
    eye = (ri == ci).astype(F32)
    bdg = (_iota((gw, gw), 0) >> sl_h) == (_iota((gw, gw), 1) >> sl_h)
    bdg16 = bdg.astype(BF16)

    for gi in range(ngroups):
        sl = slice(gi * gw, (gi + 1) * gw)
        la = stack(a_hat[:, sl]).astype(BF16)
        lr = stack(r_hat[:, sl]).astype(BF16)
        rb = tile(b_hat[:, sl]).astype(BF16)
        rkh = tile(k_hat[:, sl]).astype(BF16)
        nmat = jnp.where(strict, _dot_nt(la, rb), 0.0)
        m_ak = jnp.where(strict, _dot_nt(la, rkh), 0.0).astype(BF16)
        m_rb = jnp.where(incl, _dot_nt(lr, rb), 0.0).astype(BF16)
        m_rk = jnp.where(incl, _dot_nt(lr, rkh), 0.0).astype(BF16)
        tinv = eye + nmat
        pw = nmat
        for _ in range(sl_l - 1):
            pb = pw.astype(BF16)
            pw = _dot(pb, pb)
            tinv = tinv + _dot(tinv.astype(BF16), pw.astype(BF16))
        s_g = s_ref[gi]
        q = _dot_nt(jnp.concatenate([a_hat[:, sl], r_hat[:, sl]], axis=0).astype(BF16), s_g.astype(BF16))
        v_s = stack(v[:, sl]).astype(BF16)
        rhs = stack(q[:L]) + _dot(m_ak, v_s)
        u_s = _dot(tinv.astype(BF16), rhs.astype(BF16))
        y = q[L:] + unstack(_dot(m_rb, u_s.astype(BF16)) + _dot(m_rk, v_s))
        u_c = unstack(u_s)
        upd = _dot_tn(jnp.concatenate([u_c, v[:, sl]], axis=0).astype(BF16),
                      jnp.concatenate([b_til[:, sl], k_til[:, sl]], axis=0).astype(BF16))
        s_ref[gi] = s_g * gam_l[:, sl] + jnp.where(bdg, upd, 0.0)
        inv_n = 1.0 / RW_HEAD
        mu = _dot_sel(y, bdg16) * inv_n
        dlt = y - mu
        var = _dot_sel(dlt * dlt, bdg16) * inv_n
        yn = dlt * lax.rsqrt(var + RW_GN_EPS) * lnw_ref[:, sl] + lnb_ref[:, sl]
        bonus = _dot_sel(bonus_in[:, sl], bdg16) * v[:, sl]
        o_ref[0, :, sl] = (yn + bonus) * g_ref[0, :, sl]

    @pl.when(t == pl.num_programs(1) - 1)
    def _():
        sout_ref[0] = s_ref[...]


def rwkv_scan(streams, r_k, ln_w, ln_b, s0_bd, *, chunk=64):
    b, t, d = streams[0].shape
    L = min(chunk, t)
    has_state = s0_bd is not None
    ngroups = RW_HEADS // RW_GROUP
    gw = RW_GROUP * RW_HEAD
    tok = pl.BlockSpec((1, L, d), lambda i, j: (i, j, 0))
    vec = pl.BlockSpec((1, d), lambda i, j: (0, 0))
    st_spec = pl.BlockSpec((1, ngroups, gw, gw), lambda i, j: (i, 0, 0, 0))
    args = list(streams) + [r_k.reshape(1, d), ln_w.reshape(1, d), ln_b.reshape(1, d)]
    in_specs = [tok] * 7 + [vec] * 3
    if has_state:
        args.append(s0_bd)
        in_specs.append(st_spec)
    return pl.pallas_call(
        functools.partial(_rwkv_scan_body, L=L, has_state=has_state),
        grid=(b, t // L),
        in_specs=in_specs,
        out_specs=[tok, st_spec],
        out_shape=[jax.ShapeDtypeStruct((b, t, d), F32),
                   jax.ShapeDtypeStruct((b, ngroups, gw, gw), F32)],
        scratch_shapes=[pltpu.VMEM((ngroups, gw, gw), F32)],
        compiler_params=_cparams("parallel", "arbitrary"),
        name="rwkv_scan",
    )(*args)


def _rwkv_state_to_bd(s):
    b = s.shape[0]
    g, m, hd = RW_HEADS // RW_GROUP, RW_GROUP, RW_HEAD
    s = s.reshape(b, g, m, hd, 1, hd) * jnp.eye(m, dtype=s.dtype)[None, None, :, None, :, None]
    return s.reshape(b, g, m * hd, m * hd)


def _rwkv_state_from_bd(sbd):
    b = sbd.shape[0]
    g, m, hd = RW_HEADS // RW_GROUP, RW_GROUP, RW_HEAD
    s = sbd.reshape(b, g, m, hd, m, hd)
    s = jnp.stack([s[:, :, i, :, i, :] for i in range(m)], axis=2)
    return s.reshape(b, RW_HEADS, hd, hd)


def _lru_body(*refs, tc, has_state):
    if has_state:
        (gate_ref, xb_ref, cw_ref, cb_ref, wg_ref, bg_ref, lam_ref, h0_ref, c0_ref,
         y_ref, hout_ref, xpad_ref, hc_ref) = refs
    else:
        (gate_ref, xb_ref, cw_ref, cb_ref, wg_ref, bg_ref, lam_ref,
         y_ref, hout_ref, xpad_ref, hc_ref) = refs
    t = pl.program_id(1)
    d = D_MODEL
    pad = 8

    @pl.when(t == 0)
    def _():
        xpad_ref[0:pad, :] = c0_ref[0] if has_state else jnp.zeros((pad, d), F32)
        hc_ref[...] = h0_ref[0] if has_state else jnp.zeros((1, d), F32)

    xpad_ref[pad:pad + tc, :] = xb_ref[0]
    xc = cb_ref[...]
    for j in range(CONV_W):
        off = pad - (CONV_W - 1) + j
        xc = xc + xpad_ref[off:off + tc, :] * cw_ref[j:j + 1, :]
    xpad_ref[0:pad, :] = xpad_ref[tc:tc + pad, :]

    xcb = xc.astype(BF16)
    r_pre, i_pre = [], []
    for nb in range(LRU_BLOCKS):
        sl = slice(nb * LRU_BW, (nb + 1) * LRU_BW)
        r_pre.append(_dot(xcb[:, sl], wg_ref[0, nb]))
        i_pre.append(_dot(xcb[:, sl], wg_ref[1, nb]))
    r_gate = jax.nn.sigmoid(jnp.concatenate(r_pre, axis=-1) + bg_ref[0:1, :])
    i_gate = jax.nn.sigmoid(jnp.concatenate(i_pre, axis=-1) + bg_ref[1:2, :])
    log_a = -LRU_C * r_gate * _softplus(-lam_ref[...])
    a = jnp.exp(log_a)
    bv = jnp.sqrt(-jnp.tanh(log_a) * (a * a + 1.0)) * (i_gate * xc)

    row = _iota((tc, 1), 0)
    s = 1
    while s < tc:
        keep = row >= s
        a_sh = jnp.where(keep, pltpu.roll(a, s, axis=0), 1.0)
        b_sh = jnp.where(keep, pltpu.roll(bv, s, axis=0), 0.0)
        bv = bv + a * b_sh
        a = a * a_sh
        s *= 2
    hs = bv + a * hc_ref[...]
    last = hs[tc - 1:tc, :]
    hc_ref[...] = last
    hout_ref[0] = last
    y_ref[0] = hs * jax.nn.gelu(gate_ref[0], approximate=True)


def lru_scan(proj, conv_w, conv_b, w_gates, b_gates, lam, h0, conv0, *, tc=256):
    b, t, _ = proj.shape
    d = D_MODEL
    tc = min(tc, t)
    has_state = h0 is not None
    full = lambda a: pl.BlockSpec(a.shape, lambda i, j, n=a.ndim: (0,) * n)
    consts = [conv_w, conv_b.reshape(1, d), w_gates, b_gates, lam.reshape(1, d)]
    args = [proj, proj] + consts
    in_specs = [pl.BlockSpec((1, tc, d), lambda i, j: (i, j, 0)),
                pl.BlockSpec((1, tc, d), lambda i, j: (i, j, 1))] + [full(a) for a in consts]
    if has_state:
        c0 = jnp.pad(conv0, ((0, 0), (8 - (CONV_W - 1), 0), (0, 0)))
        args += [h0.reshape(b, 1, d), c0]
        in_specs += [pl.BlockSpec((1, 1, d), lambda i, j: (i, 0, 0)),
                     pl.BlockSpec((1, 8, d), lambda i, j: (i, 0, 0))]
    y, hl = pl.pallas_call(
        functools.partial(_lru_body, tc=tc, has_state=has_state),
        grid=(b, t // tc),
        in_specs=in_specs,
        out_specs=[pl.BlockSpec((1, tc, d), lambda i, j: (i, j, 0)),
                   pl.BlockSpec((1, 1, d), lambda i, j: (i, 0, 0))],
        out_shape=[jax.ShapeDtypeStruct((b, t, d), F32), jax.ShapeDtypeStruct((b, 1, d), F32)],
        scratch_shapes=[pltpu.VMEM((tc + 8, d), F32), pltpu.VMEM((1, d), F32)],
        compiler_params=_cparams("parallel", "arbitrary"),
        name="lru_scan",
    )(*args)
    return y, hl.reshape(b, d)


def _trunk(x, pos0, st, p):
    b, t, d = x.shape
    m = b * t
    get = (lambda name: st[name][0]) if st is not None else (lambda name: None)
    x2 = x.reshape(m, d)

    proj = norm_matmul(x2, p['norm_mix'][0], p['hg_w_in']).reshape(b, t, -1)
    o, hg_s = hgrn_scan(proj, p['hg_lb'], p['hg_norm'], get('hgrn'), layer=0)
    x2 = mix_out_ffn(o.reshape(m, d), p['hg_w_out'], x2, p['norm_ffn'][0], p['ffn_w_in'][0], p['ffn_w_out'][0],
                     p['norm_final'], final_norm=False)

    proj = norm_matmul(x2, p['norm_mix'][1], p['ret_w_in']).reshape(b, t, -1)
    cos, sin = rope_table(pos0, t)
    o, ret_s = ret_scan(proj, cos, sin, p['ret_norm'], get('ret'))
    x2 = mix_out_ffn(o.reshape(m, -1), p['ret_w_out'], x2, p['norm_ffn'][1], p['ffn_w_in'][1], p['ffn_w_out'][1],
                     p['norm_final'], final_norm=False)

    s0 = get('rwkv')
    streams, shift = rwkv_proj(x2.reshape(b, t, d), p['norm_mix'][2], get('shift'), p['rw'])
    o, rw_bd = rwkv_scan(streams, p['rw']['r_k'], p['rw']['ln_w'], p['rw']['ln_b'],
                         None if s0 is None else _rwkv_state_to_bd(s0))
    rw_s = _rwkv_state_from_bd(rw_bd)
    x2 = mix_out_ffn(o.reshape(m, d), p['rw']['w_out'], x2, p['norm_ffn'][2], p['ffn_w_in'][2], p['ffn_w_out'][2],
                     p['norm_final'], final_norm=False)

    proj = norm_matmul(x2, p['norm_mix'][3], p['lru_w_in']).reshape(b, t, -1)
    o, lru_s = lru_scan(proj, p['lru_conv_w'], p['lru_conv_b'], p['lru_w_gates'], p['lru_b_gates'],
                        p['lru_lambda'], get('lru'), get('conv'))
    if st is None:
        conv_s = proj[:, t - (CONV_W - 1):, d:]
    else:
        conv_s = jnp.concatenate([st['conv'][0], proj[:, :, d:]], axis=1)[:, -(CONV_W - 1):]
    y = mix_out_ffn(o.reshape(m, d), p['lru_w_out'], x2, p['norm_ffn'][3], p['ffn_w_in'][3], p['ffn_w_out'][3],
                    p['norm_final'], final_norm=True)
    return (y.reshape(b, t, d), hg_s[None], ret_s[None], rw_s[None], shift[None], lru_s[None], conv_s[None])


def kernel(x_prompt, x_sample, state_hgrn, state_ret, state_rwkv, state_rwkv_shift, state_lru, state_lru_conv, norm_mix, norm_ffn, norm_final, hg_lb, hg_w_in, hg_norm, hg_w_out, ret_w_in, ret_norm, ret_w_out, rw_mu, rw_w_rkv, rw_w0, rw_w1, rw_w2, rw_a0, rw_a1, rw_a2, rw_g1, rw_g2, rw_k_k, rw_k_a, rw_r_k, rw_ln_w, rw_ln_b, rw_w_out, lru_w_in, lru_conv_w, lru_conv_b, lru_w_gates, lru_b_gates, lru_lambda, lru_w_out, ffn_w_in, ffn_w_out):
    bf = lambda a: a.astype(BF16)
    p = dict(
        norm_mix=norm_mix, norm_ffn=norm_ffn, norm_final=norm_final,
        hg_lb=hg_lb, hg_w_in=bf(hg_w_in[0]), hg_norm=hg_norm[0], hg_w_out=bf(hg_w_out[0]),
        ret_w_in=bf(ret_w_in[0]), ret_norm=ret_norm[0], ret_w_out=bf(ret_w_out[0]),
        rw=dict(mu=rw_mu[0], w_rkv=bf(rw_w_rkv[0]), w0=rw_w0[0], w1=bf(rw_w1[0]), w2=bf(rw_w2[0]),
                a0=rw_a0[0], a1=bf(rw_a1[0]), a2=bf(rw_a2[0]), g1=bf(rw_g1[0]), g2=bf(rw_g2[0]),
                k_k=rw_k_k[0], k_a=rw_k_a[0], r_k=rw_r_k[0], ln_w=rw_ln_w[0], ln_b=rw_ln_b[0],
                w_out=bf(rw_w_out[0])),
        lru_w_in=bf(lru_w_in[0]), lru_conv_w=lru_conv_w[0], lru_conv_b=lru_conv_b[0],
        lru_w_gates=bf(lru_w_gates[0]), lru_b_gates=lru_b_gates[0], lru_lambda=lru_lambda[0],
        lru_w_out=bf(lru_w_out[0]),
        ffn_w_in=bf(ffn_w_in), ffn_w_out=bf(ffn_w_out),
    )
    yp, hg_p, ret_p, rw_p, sh_p, lru_p, conv_p = _trunk(x_prompt, 0, None, p)
    st = dict(hgrn=state_hgrn, ret=state_ret, rwkv=state_rwkv, shift=state_rwkv_shift,
              lru=state_lru, conv=state_lru_conv)
    ys, hg_s, ret_s, rw_s, sh_s, lru_s, conv_s = _trunk(x_sample, PAST_LEN, st, p)
    return (yp, ys, hg_p, hg_s, ret_p, ret_s, rw_p, rw_s, sh_p, sh_s, lru_p, lru_s, conv_p, conv_s)
```

```python
import functools
import math

import jax
import jax.numpy as jnp
from jax import lax
from jax.experimental import pallas as pl
from jax.experimental.pallas import tpu as pltpu

F32 = jnp.float32
BF16 = jnp.bfloat16

D_MODEL = 1024
NORM_EPS = 1e-6
HG_HEADS, HG_DK, HG_BLOCK = 8, 128, 16
RET_HEADS, RET_DK, RET_DV = 4, 256, 512
ROPE_BASE = 10000.0
RW_HEADS, RW_HEAD = 16, 64
RW_GROUP = 4
RW_GN_EPS = 64e-5
LRU_BLOCKS, LRU_BW, CONV_W, LRU_C = 8, 128, 4, 8.0
D_FF = 2816
PAST_LEN = 2048

VMEM_LIMIT_BYTES = 56 * 1024 * 1024


def _cparams(*sem):
    return pltpu.CompilerParams(dimension_semantics=sem, vmem_limit_bytes=VMEM_LIMIT_BYTES)


def _const_spec(shape):
    n = len(shape)
    return pl.BlockSpec(tuple(shape), lambda *_: (0,) * n, pipeline_mode=pl.Buffered(1))


def _iota(shape, dim):
    return lax.broadcasted_iota(jnp.int32, shape, dim)


def _dot(a, b):
    return jnp.dot(a, b, preferred_element_type=F32)


def _dot_nt(a, b):
    return lax.dot_general(a, b, (((1,), (1,)), ((), ())), preferred_element_type=F32)


def _dot_tn(a, b):
    return lax.dot_general(a, b, (((0,), (0,)), ((), ())), preferred_element_type=F32)


def _split3(x):
    hi = x.astype(BF16)
    r1 = x - hi.astype(F32)
    mid = r1.astype(BF16)
    lo = (r1 - mid.astype(F32)).astype(BF16)
    return hi, mid, lo


def _sel_dot(m01, x):
    hi, mid, lo = _split3(x)
    return _dot(m01, hi) + _dot(m01, mid) + _dot(m01, lo)


def _dot_sel(x, m01):
    hi, mid, lo = _split3(x)
    return _dot(hi, m01) + _dot(mid, m01) + _dot(lo, m01)


def _rms(x, w):
    return x * lax.rsqrt(jnp.mean(x * x, axis=-1, keepdims=True) + NORM_EPS) * w


def _softplus(x):
    return jnp.maximum(x, 0.0) + jnp.log1p(jnp.exp(-jnp.abs(x)))


def _log2(n):
    l = int(math.log2(n))
    assert 1 << l == n, n
    return l


def _norm_mm_body(x_ref, nw_ref, w_ref, o_ref, *, tn):
    h = _rms(x_ref[...], nw_ref[...]).astype(BF16)
    for c in range(w_ref.shape[1] // tn):
        o_ref[:, c * tn:(c + 1) * tn] = _dot(h, w_ref[:, c * tn:(c + 1) * tn])


def norm_matmul(x2d, nw, w, *, tm=512, tn=1024):
    m, k = x2d.shape
    n = w.shape[1]
    tm, tn = min(tm, m), min(tn, n)
    return pl.pallas_call(
        functools.partial(_norm_mm_body, tn=tn),
        grid=(m // tm,),
        in_specs=[pl.BlockSpec((tm, k), lambda i: (i, 0)),
                  _const_spec((1, k)),
                  _const_spec((k, n))],
        out_specs=pl.BlockSpec((tm, n), lambda i: (i, 0)),
        out_shape=jax.ShapeDtypeStruct((m, n), F32),
        compiler_params=_cparams("parallel"),
        name="norm_matmul",
    )(x2d, nw.reshape(1, k), w)


def _mix_ffn_body(o_ref, wo_ref, x_ref, nw_ref, win_ref, wout_ref, fnw_ref, y_ref, *, final_norm):
    x1 = x_ref[...] + _dot(o_ref[...].astype(BF16), wo_ref[...])
    h = _rms(x1, nw_ref[...]).astype(BF16)
    half = D_FF // 2
    g = [_dot(h, win_ref[:, c * half:(c + 1) * half]) for c in range(2)]
    u = [_dot(h, win_ref[:, D_FF + c * half:D_FF + (c + 1) * half]) for c in range(2)]
    act = [(g[c] * jax.nn.sigmoid(g[c]) * u[c]).astype(BF16) for c in range(2)]
    acc = x1 + _dot(act[0], wout_ref[0:half, :]) + _dot(act[1], wout_ref[half:D_FF, :])
    if final_norm:
        acc = _rms(acc, fnw_ref[...])
    y_ref[...] = acc


def mix_out_ffn(o2d, w_o, x2d, nw, w_in, w_out, fnw, *, final_norm, tm=512):
    m, ko = o2d.shape
    d = x2d.shape[1]
    tm = min(tm, m)
    return pl.pallas_call(
        functools.partial(_mix_ffn_body, final_norm=final_norm),
        grid=(m // tm,),
        in_specs=[pl.BlockSpec((tm, ko), lambda i: (i, 0)),
                  _const_spec((ko, d)),
                  pl.BlockSpec((tm, d), lambda i: (i, 0)),
                  _const_spec((1, d)),
                  _const_spec((d, 2 * D_FF)),
                  _const_spec((D_FF, d)),
                  _const_spec((1, d))],
        out_specs=pl.BlockSpec((tm, d), lambda i: (i, 0)),
        out_shape=jax.ShapeDtypeStruct((m, d), F32),
        compiler_params=_cparams("parallel"),
        name="mix_out_ffn",
    )(o2d, w_o, x2d, nw.reshape(1, d), w_in, w_out, fnw.reshape(1, d))


def _hgrn_body(*refs, tc, layer, has_state):
    if has_state:
        (q_ref, f_ref, v_ref, g_ref, lbp_ref, gn_ref, s0_ref, o_ref, sout_ref,
         st_ref, qin_ref, kst_ref, dec_ref, oacc_ref) = refs
    else:
        (q_ref, f_ref, v_ref, g_ref, lbp_ref, gn_ref, o_ref, sout_ref,
         st_ref, qin_ref, kst_ref, dec_ref, oacc_ref) = refs
    t = pl.program_id(1)
    blk = min(HG_BLOCK, tc)
    heads, dk = HG_HEADS, HG_DK

    @pl.when(t == 0)
    def _():
        for h in range(heads):
            st_ref[h] = s0_ref[0, h].T if has_state else jnp.zeros((dk, dk), F32)

    lbp = lbp_ref[...]
    e = jnp.exp(lbp - jnp.max(lbp, axis=0, keepdims=True))
    lb = jnp.sum(e[:layer + 1], axis=0, keepdims=True) / jnp.sum(e, axis=0, keepdims=True)

    fl = f_ref[0]
    f = lb + (1.0 - lb) * jax.nn.sigmoid(fl)
    k = (1.0 - lb) * jax.nn.sigmoid(-fl)
    lf = jnp.log(f)
    sh = _log2(blk)
    ri, ci = _iota((tc, tc), 0), _iota((tc, tc), 1)
    same = (ri >> sh) == (ci >> sh)
    causal = same & (ci <= ri)
    cum = _sel_dot(causal.astype(BF16), lf)
    tot = _sel_dot(same.astype(BF16), lf)
    q_in = q_ref[0] * jnp.exp(cum)
    k_in = k * jnp.exp(-cum)
    qin_ref[...] = q_in
    kst_ref[...] = k * jnp.exp(tot - cum)
    dec_ref[...] = jnp.exp(tot)
    v = v_ref[0]
    sls = [slice(h * dk, (h + 1) * dk) for h in range(heads)]
    s = [_dot_nt(q_in[:, sl].astype(BF16), k_in[:, sl].astype(BF16)) for sl in sls]
    s = [jnp.where(causal, x, 0.0).astype(BF16) for x in s]
    for h, sl in enumerate(sls):
        oacc_ref[:, sl] = _dot(s[h], v[:, sl].astype(BF16))

    def sub(i, carry):
        r0 = pl.multiple_of(i * blk, blk)
        rows = pl.ds(r0, blk)
        st = [st_ref[h] for h in range(heads)]
        o_in = [_dot_nt(qin_ref[rows, sl].astype(BF16), st[h].astype(BF16)) for h, sl in enumerate(sls)]
        kv = [_dot_tn(v_ref[0, rows, sl].astype(BF16), kst_ref[rows, sl].astype(BF16)) for sl in sls]
        for h, sl in enumerate(sls):
            oacc_ref[rows, sl] += o_in[h]
            st_ref[h] = st[h] * dec_ref[pl.ds(r0, 1), sl] + kv[h]
        return carry

    lax.fori_loop(0, tc // blk, sub, 0)

    o = oacc_ref[...]
    g = g_ref[0]
    gn = gn_ref[...]
    for h in range(heads):
        sl = slice(h * dk, (h + 1) * dk)
        oh = o[:, sl]
        oh = oh * lax.rsqrt(jnp.mean(oh * oh, axis=-1, keepdims=True) + NORM_EPS)
        gh = g[:, sl]
        o_ref[0, :, sl] = (oh * gn[:, sl] * (gh * jax.nn.sigmoid(gh))).astype(BF16)

    @pl.when(t == pl.num_programs(1) - 1)
    def _():
        for h in range(heads):
            sout_ref[0, h] = st_ref[h].T


def hgrn_scan(proj, hg_lb, g_norm, s0, *, layer, tc=128):
    b, t, _ = proj.shape
    d = D_MODEL
    tc = min(tc, t)
    has_state = s0 is not None
    col = lambda c: pl.BlockSpec((1, tc, d), lambda i, j, c=c: (i, j, c))
    in_specs = [col(0), col(1), col(2), col(3),
                pl.BlockSpec(hg_lb.shape, lambda i, j: (0, 0)),
                pl.BlockSpec((1, d), lambda i, j: (0, 0))]
    args = [proj, proj, proj, proj, hg_lb, g_norm.reshape(1, d)]
    st_spec = pl.BlockSpec((1, HG_HEADS, HG_DK, HG_DK), lambda i, j: (i, 0, 0, 0))
    if has_state:
        in_specs.append(st_spec)
        args.append(s0)
    return pl.pallas_call(
        functools.partial(_hgrn_body, tc=tc, layer=layer, has_state=has_state),
        grid=(b, t // tc),
        in_specs=in_specs,
        out_specs=[pl.BlockSpec((1, tc, d), lambda i, j: (i, j, 0)), st_spec],
        out_shape=[jax.ShapeDtypeStruct((b, t, d), BF16),
                   jax.ShapeDtypeStruct((b, HG_HEADS, HG_DK, HG_DK), F32)],
        scratch_shapes=[pltpu.VMEM((HG_HEADS, HG_DK, HG_DK), F32)] + [pltpu.VMEM((tc, d), F32)] * 4,
        compiler_params=_cparams("parallel", "arbitrary"),
        name="hgrn_scan",
    )(*args)


def _rope_body(cos_ref, sin_ref, *, pos0, tc):
    half = RET_DK // 2
    pos = (pos0 + pl.program_id(0) * tc + _iota((tc, half), 0)).astype(F32)
    inv = jnp.power(jnp.float32(ROPE_BASE), -(_iota((tc, half), 1).astype(F32) / half))
    ang = pos * inv
    cos_ref[...] = jnp.cos(ang)
    sin_ref[...] = jnp.sin(ang)


def rope_table(pos0, t, *, tc=256):
    tc = min(tc, t)
    half = RET_DK // 2
    return pl.pallas_call(
        functools.partial(_rope_body, pos0=pos0, tc=tc),
        grid=(t // tc,),
        out_specs=[pl.BlockSpec((tc, half), lambda i: (i, 0))] * 2,
        out_shape=[jax.ShapeDtypeStruct((t, half), F32)] * 2,
        compiler_params=_cparams("parallel"),
        name="rope_table",
    )()


def _ret_body(*refs, tc, has_state):
    if has_state:
        q_ref, k_ref, v_ref, g_ref, cos_ref, sin_ref, gn_ref, s0_ref, o_ref, sout_ref, s_ref = refs
    else:
        q_ref, k_ref, v_ref, g_ref, cos_ref, sin_ref, gn_ref, o_ref, sout_ref, s_ref = refs
    t = pl.program_id(1)
    nh, dk, dv = RET_HEADS, RET_DK, RET_DV
    half = dk // 2

    @pl.when(t == 0)
    def _():
        s_ref[...] = s0_ref[0] if has_state else jnp.zeros((nh, dk, dv), F32)

    cos, sin = cos_ref[...], sin_ref[...]

    def rot(x):
        x1, x2 = x[:, :half], x[:, half:]
        return jnp.concatenate([x1 * cos - x2 * sin, x1 * sin + x2 * cos], axis=-1)

    heads = range(nh)
    lg = [math.log1p(-2.0 ** (-5.0 - h)) for h in heads]
    ksl = [slice(h * dk, (h + 1) * dk) for h in heads]
    vsl = [slice(h * dv, (h + 1) * dv) for h in heads]
    diff = (_iota((tc, tc), 0) - _iota((tc, tc), 1)).astype(F32)
    row = _iota((tc, 1), 0).astype(F32)
    q = [rot(q_ref[0, :, sl]) for sl in ksl]
    k = [rot(k_ref[0, :, sl]) * (dk ** -0.5) for sl in ksl]
    v = [v_ref[0, :, sl].astype(BF16) for sl in vsl]
    s = [s_ref[h] for h in heads]
    sc = [_dot_nt(q[h].astype(BF16), k[h].astype(BF16)) for h in heads]
    sc = [(jnp.where(diff >= 0.0, jnp.exp(diff * lg[h]), 0.0) * sc[h]).astype(BF16) for h in heads]
    o = [_dot(sc[h], v[h]) + _dot((q[h] * jnp.exp((row + 1.0) * lg[h])).astype(BF16), s[h].astype(BF16))
         for h in heads]
    kv = [_dot_tn((k[h] * jnp.exp((tc - 1.0 - row) * lg[h])).astype(BF16), v[h]) for h in heads]
    for h in heads:
        s_ref[h] = s[h] * math.exp(tc * lg[h]) + kv[h]
        oh = o[h] * lax.rsqrt(jnp.mean(o[h] * o[h], axis=-1, keepdims=True) + NORM_EPS)
        g = g_ref[0, :, vsl[h]]
        o_ref[0, :, vsl[h]] = (oh * gn_ref[:, vsl[h]] * (g * jax.nn.sigmoid(g))).astype(BF16)

    @pl.when(t == pl.num_programs(1) - 1)
    def _():
        sout_ref[0] = s_ref[...]


def ret_scan(proj, cos, sin, g_norm, s0, *, tc=256):
    b, t, _ = proj.shape
    tc = min(tc, t)
    has_state = s0 is not None
    nh, dk, dv = RET_HEADS, RET_DK, RET_DV
    half = dk // 2
    in_specs = [pl.BlockSpec((1, tc, nh * dk), lambda i, j: (i, j, 0)),
                pl.BlockSpec((1, tc, nh * dk), lambda i, j: (i, j, 1)),
                pl.BlockSpec((1, tc, nh * dv), lambda i, j: (i, j, 1)),
                pl.BlockSpec((1, tc, nh * dv), lambda i, j: (i, j, 2)),
                pl.BlockSpec((tc, half), lambda i, j: (j, 0)),
                pl.BlockSpec((tc, half), lambda i, j: (j, 0)),
                _const_spec((1, nh * dv))]
    args = [proj, proj, proj, proj, cos, sin, g_norm.reshape(1, nh * dv)]
    st_spec = pl.BlockSpec((1, nh, dk, dv), lambda i, j: (i, 0, 0, 0))
    if has_state:
        in_specs.append(st_spec)
        args.append(s0)
    return pl.pallas_call(
        functools.partial(_ret_body, tc=tc, has_state=has_state),
        grid=(b, t // tc),
        in_specs=in_specs,
        out_specs=[pl.BlockSpec((1, tc, nh * dv), lambda i, j: (i, j, 0)), st_spec],
        out_shape=[jax.ShapeDtypeStruct((b, t, nh * dv), BF16),
                   jax.ShapeDtypeStruct((b, nh, dk, dv), F32)],
        scratch_shapes=[pltpu.VMEM((nh, dk, dv), F32)],
        compiler_params=_cparams("parallel", "arbitrary"),
        name="ret_scan",
    )(*args)


def _rwkv_proj_body(*refs, has_state):
    if has_state:
        (x_ref, nw_ref, sh0_ref, mu_ref, wrkv_ref, w0_ref, w1_ref, w2_ref, a0_ref, a1_ref, a2_ref,
         g1_ref, g2_ref, kk_ref, ka_ref, hsum_ref, hexp_ref,
         r_o, k_o, v_o, kk_o, a_o, lw_o, g_o, sh_o, carry_ref) = refs
    else:
        (x_ref, nw_ref, mu_ref, wrkv_ref, w0_ref, w1_ref, w2_ref, a0_ref, a1_ref, a2_ref,
         g1_ref, g2_ref, kk_ref, ka_ref, hsum_ref, hexp_ref,
         r_o, k_o, v_o, kk_o, a_o, lw_o, g_o, sh_o, carry_ref) = refs
    t = pl.program_id(1)
    tm = x_ref.shape[1]

    @pl.when(t == 0)
    def _():
        carry_ref[...] = sh0_ref[0] if has_state else jnp.zeros((1, D_MODEL), F32)

    h = _rms(x_ref[0], nw_ref[...])
    prev = jnp.where(_iota((tm, 1), 0) == 0, carry_ref[...], pltpu.roll(h, 1, axis=0))
    last = h[tm - 1:tm, :]
    carry_ref[...] = last
    sh_o[0] = last
    d = prev - h
    mix = lambda i: (h + d * mu_ref[i:i + 1, :]).astype(BF16)
    r = _dot(mix(0), wrkv_ref[0])
    k = _dot(mix(1), wrkv_ref[1])
    v = _dot(mix(2), wrkv_ref[2])
    w_pre = w0_ref[...] + _dot(jnp.tanh(_dot(mix(3), w1_ref[...])).astype(BF16), w2_ref[...])
    lw_o[0] = -jnp.exp(-_softplus(-w_pre) - 0.5)
    a = jax.nn.sigmoid(a0_ref[...] + _dot(_dot(mix(4), a1_ref[...]).astype(BF16), a2_ref[...]))
    g_o[0] = _dot(jax.nn.sigmoid(_dot(mix(5), g1_ref[...])).astype(BF16), g2_ref[...])
    kk = k * kk_ref[...]
    ss = _dot_sel(kk * kk, hsum_ref[...])
    inv = 1.0 / jnp.maximum(jnp.sqrt(ss), 1e-12)
    kk_o[0] = kk * _dot_sel(inv, hexp_ref[...])
    r_o[0] = r
    k_o[0] = k * (1.0 + (a - 1.0) * ka_ref[...])
    v_o[0] = v
    a_o[0] = a


def rwkv_proj(x, nw, shift0, p, *, tm=256):
    b, t, d = x.shape
    tm = min(tm, t)
    has_state = shift0 is not None
    row = lambda a: a.reshape(1, d)
    head_of_lane = jnp.arange(d) // RW_HEAD
    hsum = (head_of_lane[:, None] == jnp.arange(RW_HEADS)[None, :]).astype(BF16)
    hexp = hsum.T
    full = lambda a: pl.BlockSpec(a.shape, lambda i, j, n=a.ndim: (0,) * n)
    tok = pl.BlockSpec((1, tm, d), lambda i, j: (i, j, 0))
    args = [x, row(nw)]
    in_specs = [tok, full(row(nw))]
    if has_state:
        args.append(shift0.reshape(b, 1, d))
        in_specs.append(pl.BlockSpec((1, 1, d), lambda i, j: (i, 0, 0)))
    consts = [p['mu'], p['w_rkv'], row(p['w0']), p['w1'], p['w2'], row(p['a0']), p['a1'], p['a2'],
              p['g1'], p['g2'], row(p['k_k']), row(p['k_a']), hsum, hexp]
    args += consts
    in_specs += [full(a) for a in consts]
    outs = pl.pallas_call(
        functools.partial(_rwkv_proj_body, has_state=has_state),
        grid=(b, t // tm),
        in_specs=in_specs,
        out_specs=[tok] * 7 + [pl.BlockSpec((1, 1, d), lambda i, j: (i, 0, 0))],
        out_shape=[jax.ShapeDtypeStruct((b, t, d), F32)] * 7 + [jax.ShapeDtypeStruct((b, 1, d), F32)],
        scratch_shapes=[pltpu.VMEM((1, d), F32)],
        compiler_params=_cparams("parallel", "arbitrary"),
        name="rwkv_proj",
    )(*args)
    return outs[:7], outs[7].reshape(b, d)


def _rwkv_scan_body(*refs, L, has_state):
    if has_state:
        (r_ref, k_ref, v_ref, kk_ref, a_ref, lw_ref, g_ref, rk_ref, lnw_ref, lnb_ref, s0_ref,
         o_ref, sout_ref, s_ref) = refs
    else:
        (r_ref, k_ref, v_ref, kk_ref, a_ref, lw_ref, g_ref, rk_ref, lnw_ref, lnb_ref,
         o_ref, sout_ref, s_ref) = refs
    t = pl.program_id(1)
    gw = RW_GROUP * RW_HEAD
    ngroups = RW_HEADS // RW_GROUP
    sl_l, sl_h = _log2(L), _log2(RW_HEAD)
    n = RW_GROUP * L

    @pl.when(t == 0)
    def _():
        s_ref[...] = s0_ref[0] if has_state else jnp.zeros(s_ref.shape, F32)

    lw = lw_ref[0]
    tri = (_iota((L, L), 1) <= _iota((L, L), 0)).astype(BF16)
    cum = _sel_dot(tri, lw)
    cum_l = cum[L - 1:L, :]
    e_pos, e_neg = jnp.exp(cum), jnp.exp(-cum)
    e_tail = jnp.exp(cum_l - cum)
    kk, a, r, k, v = kk_ref[0], a_ref[0], r_ref[0], k_ref[0], v_ref[0]
    beta = kk * a
    a_hat = -kk * jnp.exp(cum - lw)
    r_hat = r * e_pos
    b_hat = beta * e_neg
    k_hat = k * e_neg
    b_til = beta * e_tail
    k_til = k * e_tail
    gam_l = jnp.exp(cum_l)
    bonus_in = r * k * rk_ref[...]

    own = (_iota((n, 1), 0) >> sl_l) == (_iota((1, gw), 1) >> sl_h)
    tile = lambda x: jnp.concatenate([x] * RW_GROUP, axis=0)
    stack = lambda x: jnp.where(own, tile(x), 0.0).astype(BF16)
    same = (_iota((n, n), 0) >> sl_l) == (_iota((n, n), 1) >> sl_l)
    to_bd = lambda x: jnp.where(same, tile(x), 0.0).astype(BF16)
    row_c, col_c = _iota((L, n), 0), _iota((L, n), 1) & (L - 1)
    strict_c, incl_c = col_c < row_c, col_c <= row_c
    eye_c = (col_c == row_c).astype(F32)
    bdg = (_iota((gw, gw), 0) >> sl_h) == (_iota((gw, gw), 1) >> sl_h)
    bdg16 = bdg.astype(BF16)
    inv_n = 1.0 / RW_HEAD

    groups = range(ngroups)
    sls = [slice(gi * gw, (gi + 1) * gw) for gi in groups]
    ar = [jnp.concatenate([a_hat[:, sl], r_hat[:, sl]], axis=0).astype(BF16) for sl in sls]
    c = [_dot_nt(ar[gi], jnp.concatenate([stack(b_hat[:, sl]), stack(k_hat[:, sl])], axis=0))
         for gi, sl in enumerate(sls)]
    p_c = [jnp.where(strict_c, x[:L, :n], 0.0) for x in c]
    m_ak = [jnp.where(strict_c, x[:L, n:], 0.0).astype(BF16) for x in c]
    m_r = [jnp.concatenate([jnp.where(incl_c, x[L:, :n], 0.0), jnp.where(incl_c, x[L:, n:], 0.0)],
                           axis=1).astype(BF16) for x in c]
    t_c = [eye_c + x for x in p_c]
    for j in range(sl_l):
        p_bd = [to_bd(x) for x in p_c]
        if j == 0:
            p_c = [_dot(p_c[gi].astype(BF16), p_bd[gi]) for gi in groups]
        elif j < sl_l - 1:
            both = [_dot(jnp.concatenate([p_c[gi], t_c[gi]], axis=0).astype(BF16), p_bd[gi]) for gi in groups]
            p_c = [x[:L] for x in both]
            t_c = [t_c[gi] + both[gi][L:] for gi in groups]
        else:
            t_c = [t_c[gi] + _dot(t_c[gi].astype(BF16), p_bd[gi]) for gi in groups]
    s_g = [s_ref[gi] for gi in groups]
    q = [_dot_nt(ar[gi], s_g[gi].astype(BF16)) for gi in groups]
    v_s = [stack(v[:, sl]) for sl in sls]
    rhs = [q[gi][:L] + _dot(m_ak[gi], v_s[gi]) for gi in groups]
    u_c = [_dot(t_c[gi].astype(BF16), stack(rhs[gi])) for gi in groups]
    y = [q[gi][L:] + _dot(m_r[gi], jnp.concatenate([stack(u_c[gi]), v_s[gi]], axis=0)) for gi in groups]
    for gi, sl in enumerate(sls):
        upd = _dot_tn(jnp.concatenate([u_c[gi], v[:, sl]], axis=0).astype(BF16),
                      jnp.concatenate([b_til[:, sl], k_til[:, sl]], axis=0).astype(BF16))
        s_ref[gi] = s_g[gi] * gam_l[:, sl] + jnp.where(bdg, upd, 0.0)
    for gi, sl in enumerate(sls):
        pieces = _split3(y[gi]) + _split3(y[gi] * y[gi]) + _split3(bonus_in[:, sl])
        sums = _dot(jnp.concatenate(pieces, axis=0), bdg16)
        sum3 = lambda i: sums[3 * i * L:(3 * i + 1) * L] + sums[(3 * i + 1) * L:(3 * i + 2) * L] \
            + sums[(3 * i + 2) * L:(3 * i + 3) * L]
        mu = sum3(0) * inv_n
        var = sum3(1) * inv_n - mu * mu
        yn = (y[gi] - mu) * lax.rsqrt(var + RW_GN_EPS) * lnw_ref[:, sl] + lnb_ref[:, sl]
        o_ref[0, :, sl] = ((yn + sum3(2) * v[:, sl]) * g_ref[0, :, sl]).astype(BF16)

    @pl.when(t == pl.num_programs(1) - 1)
    def _():
        sout_ref[0] = s_ref[...]


def rwkv_scan(streams, r_k, ln_w, ln_b, s0_bd, *, chunk=64):
    b, t, d = streams[0].shape
    L = min(chunk, t)
    has_state = s0_bd is not None
    ngroups = RW_HEADS // RW_GROUP
    gw = RW_GROUP * RW_HEAD
    tok = pl.BlockSpec((1, L, d), lambda i, j: (i, j, 0))
    vec = pl.BlockSpec((1, d), lambda i, j: (0, 0))
    st_spec = pl.BlockSpec((1, ngroups, gw, gw), lambda i, j: (i, 0, 0, 0))
    args = list(streams) + [r_k.reshape(1, d), ln_w.reshape(1, d), ln_b.reshape(1, d)]
    in_specs = [tok] * 7 + [vec] * 3
    if has_state:
        args.append(s0_bd)
        in_specs.append(st_spec)
    return pl.pallas_call(
        functools.partial(_rwkv_scan_body, L=L, has_state=has_state),
        grid=(b, t // L),
        in_specs=in_specs,
        out_specs=[tok, st_spec],
        out_shape=[jax.ShapeDtypeStruct((b, t, d), BF16),
                   jax.ShapeDtypeStruct((b, ngroups, gw, gw), F32)],
        scratch_shapes=[pltpu.VMEM((ngroups, gw, gw), F32)],
        compiler_params=_cparams("parallel", "arbitrary"),
        name="rwkv_scan",
    )(*args)


def _rwkv_state_to_bd(s):
    b = s.shape[0]
    g, m, hd = RW_HEADS // RW_GROUP, RW_GROUP, RW_HEAD
    s = s.reshape(b, g, m, hd, 1, hd) * jnp.eye(m, dtype=s.dtype)[None, None, :, None, :, None]
    return s.reshape(b, g, m * hd, m * hd)


def _rwkv_state_from_bd(sbd):
    b = sbd.shape[0]
    g, m, hd = RW_HEADS // RW_GROUP, RW_GROUP, RW_HEAD
    s = sbd.reshape(b, g, m, hd, m, hd)
    s = jnp.stack([s[:, :, i, :, i, :] for i in range(m)], axis=2)
    return s.reshape(b, RW_HEADS, hd, hd)


def _lru_body(*refs, tc, has_state):
    if has_state:
        (gate_ref, xb_ref, cw_ref, cb_ref, wg_ref, bg_ref, lam_ref, h0_ref, c0_ref,
         y_ref, hout_ref, xpad_ref, hc_ref) = refs
    else:
        (gate_ref, xb_ref, cw_ref, cb_ref, wg_ref, bg_ref, lam_ref,
         y_ref, hout_ref, xpad_ref, hc_ref) = refs
    t = pl.program_id(1)
    d = D_MODEL
    pad = 8

    @pl.when(t == 0)
    def _():
        xpad_ref[0:pad, :] = c0_ref[0] if has_state else jnp.zeros((pad, d), F32)
        hc_ref[...] = h0_ref[0] if has_state else jnp.zeros((1, d), F32)

    xpad_ref[pad:pad + tc, :] = xb_ref[0]
    xc = cb_ref[...]
    for j in range(CONV_W):
        off = pad - (CONV_W - 1) + j
        xc = xc + xpad_ref[off:off + tc, :] * cw_ref[j:j + 1, :]
    xpad_ref[0:pad, :] = xpad_ref[tc:tc + pad, :]

    xcb = xc.astype(BF16)
    r_pre, i_pre = [], []
    for nb in range(LRU_BLOCKS):
        sl = slice(nb * LRU_BW, (nb + 1) * LRU_BW)
        r_pre.append(_dot(xcb[:, sl], wg_ref[0, nb]))
        i_pre.append(_dot(xcb[:, sl], wg_ref[1, nb]))
    r_gate = jax.nn.sigmoid(jnp.concatenate(r_pre, axis=-1) + bg_ref[0:1, :])
    i_gate = jax.nn.sigmoid(jnp.concatenate(i_pre, axis=-1) + bg_ref[1:2, :])
    log_a = -LRU_C * r_gate * _softplus(-lam_ref[...])
    a = jnp.exp(log_a)
    bv = jnp.sqrt(-jnp.tanh(log_a) * (a * a + 1.0)) * (i_gate * xc)

    grp = 8
    a = a.reshape(tc // grp, grp, d)
    bv = bv.reshape(tc // grp, grp, d)
    sub = _iota((1, grp, 1), 1)
    s = 1
    while s < grp:
        keep = sub >= s
        a_sh = jnp.where(keep, pltpu.roll(a, s, axis=1), 1.0)
        b_sh = jnp.where(keep, pltpu.roll(bv, s, axis=1), 0.0)
        bv = bv + a * b_sh
        a = a * a_sh
        s *= 2
    carry = hc_ref[...]
    rows = []
    for gi in range(tc // grp):
        h_g = bv[gi] + a[gi] * carry
        carry = h_g[grp - 1:grp, :]
        rows.append(h_g)
    hs = jnp.concatenate(rows, axis=0)
    last = carry
    hc_ref[...] = last
    hout_ref[0] = last
    y_ref[0] = (hs * jax.nn.gelu(gate_ref[0], approximate=True)).astype(BF16)


def lru_scan(proj, conv_w, conv_b, w_gates, b_gates, lam, h0, conv0, *, tc=256):
    b, t, _ = proj.shape
    d = D_MODEL
    tc = min(tc, t)
    has_state = h0 is not None
    full = lambda a: pl.BlockSpec(a.shape, lambda i, j, n=a.ndim: (0,) * n)
    consts = [conv_w, conv_b.reshape(1, d), w_gates, b_gates, lam.reshape(1, d)]
    args = [proj, proj] + consts
    in_specs = [pl.BlockSpec((1, tc, d), lambda i, j: (i, j, 0)),
                pl.BlockSpec((1, tc, d), lambda i, j: (i, j, 1))] + [full(a) for a in consts]
    if has_state:
        c0 = jnp.pad(conv0, ((0, 0), (8 - (CONV_W - 1), 0), (0, 0)))
        args += [h0.reshape(b, 1, d), c0]
        in_specs += [pl.BlockSpec((1, 1, d), lambda i, j: (i, 0, 0)),
                     pl.BlockSpec((1, 8, d), lambda i, j: (i, 0, 0))]
    y, hl = pl.pallas_call(
        functools.partial(_lru_body, tc=tc, has_state=has_state),
        grid=(b, t // tc),
        in_specs=in_specs,
        out_specs=[pl.BlockSpec((1, tc, d), lambda i, j: (i, j, 0)),
                   pl.BlockSpec((1, 1, d), lambda i, j: (i, 0, 0))],
        out_shape=[jax.ShapeDtypeStruct((b, t, d), BF16), jax.ShapeDtypeStruct((b, 1, d), F32)],
        scratch_shapes=[pltpu.VMEM((tc + 8, d), F32), pltpu.VMEM((1, d), F32)],
        compiler_params=_cparams("parallel", "arbitrary"),
        name="lru_scan",
    )(*args)
    return y, hl.reshape(b, d)


def _trunk(x, pos0, st, p):
    b, t, d = x.shape
    m = b * t
    get = (lambda name: st[name][0]) if st is not None else (lambda name: None)
    x2 = x.reshape(m, d)

    proj = norm_matmul(x2, p['norm_mix'][0], p['hg_w_in']).reshape(b, t, -1)
    o, hg_s = hgrn_scan(proj, p['hg_lb'], p['hg_norm'], get('hgrn'), layer=0)
    x2 = mix_out_ffn(o.reshape(m, d), p['hg_w_out'], x2, p['norm_ffn'][0], p['ffn_w_in'][0], p['ffn_w_out'][0],
                     p['norm_final'], final_norm=False)

    proj = norm_matmul(x2, p['norm_mix'][1], p['ret_w_in']).reshape(b, t, -1)
    cos, sin = rope_table(pos0, t)
    o, ret_s = ret_scan(proj, cos, sin, p['ret_norm'], get('ret'))
    x2 = mix_out_ffn(o.reshape(m, -1), p['ret_w_out'], x2, p['norm_ffn'][1], p['ffn_w_in'][1], p['ffn_w_out'][1],
                     p['norm_final'], final_norm=False)

    s0 = get('rwkv')
    streams, shift = rwkv_proj(x2.reshape(b, t, d), p['norm_mix'][2], get('shift'), p['rw'])
    o, rw_bd = rwkv_scan(streams, p['rw']['r_k'], p['rw']['ln_w'], p['rw']['ln_b'],
                         None if s0 is None else _rwkv_state_to_bd(s0))
    rw_s = _rwkv_state_from_bd(rw_bd)
    x2 = mix_out_ffn(o.reshape(m, d), p['rw']['w_out'], x2, p['norm_ffn'][2], p['ffn_w_in'][2], p['ffn_w_out'][2],
                     p['norm_final'], final_norm=False)

    proj = norm_matmul(x2, p['norm_mix'][3], p['lru_w_in']).reshape(b, t, -1)
    o, lru_s = lru_scan(proj, p['lru_conv_w'], p['lru_conv_b'], p['lru_w_gates'], p['lru_b_gates'],
                        p['lru_lambda'], get('lru'), get('conv'))
    if st is None:
        conv_s = proj[:, t - (CONV_W - 1):, d:]
    else:
        conv_s = jnp.concatenate([st['conv'][0], proj[:, :, d:]], axis=1)[:, -(CONV_W - 1):]
    y = mix_out_ffn(o.reshape(m, d), p['lru_w_out'], x2, p['norm_ffn'][3], p['ffn_w_in'][3], p['ffn_w_out'][3],
                    p['norm_final'], final_norm=True)
    return (y.reshape(b, t, d), hg_s[None], ret_s[None], rw_s[None], shift[None], lru_s[None], conv_s[None])


def kernel(x_prompt, x_sample, state_hgrn, state_ret, state_rwkv, state_rwkv_shift, state_lru, state_lru_conv, norm_mix, norm_ffn, norm_final, hg_lb, hg_w_in, hg_norm, hg_w_out, ret_w_in, ret_norm, ret_w_out, rw_mu, rw_w_rkv, rw_w0, rw_w1, rw_w2, rw_a0, rw_a1, rw_a2, rw_g1, rw_g2, rw_k_k, rw_k_a, rw_r_k, rw_ln_w, rw_ln_b, rw_w_out, lru_w_in, lru_conv_w, lru_conv_b, lru_w_gates, lru_b_gates, lru_lambda, lru_w_out, ffn_w_in, ffn_w_out):
    bf = lambda a: a.astype(BF16)
    p = dict(
        norm_mix=norm_mix, norm_ffn=norm_ffn, norm_final=norm_final,
        hg_lb=hg_lb, hg_w_in=bf(hg_w_in[0]), hg_norm=hg_norm[0], hg_w_out=bf(hg_w_out[0]),
        ret_w_in=bf(ret_w_in[0]), ret_norm=ret_norm[0], ret_w_out=bf(ret_w_out[0]),
        rw=dict(mu=rw_mu[0], w_rkv=bf(rw_w_rkv[0]), w0=rw_w0[0], w1=bf(rw_w1[0]), w2=bf(rw_w2[0]),
                a0=rw_a0[0], a1=bf(rw_a1[0]), a2=bf(rw_a2[0]), g1=bf(rw_g1[0]), g2=bf(rw_g2[0]),
                k_k=rw_k_k[0], k_a=rw_k_a[0], r_k=rw_r_k[0], ln_w=rw_ln_w[0], ln_b=rw_ln_b[0],
                w_out=bf(rw_w_out[0])),
        lru_w_in=bf(lru_w_in[0]), lru_conv_w=lru_conv_w[0], lru_conv_b=lru_conv_b[0],
        lru_w_gates=bf(lru_w_gates[0]), lru_b_gates=lru_b_gates[0], lru_lambda=lru_lambda[0],
        lru_w_out=bf(lru_w_out[0]),
        ffn_w_in=bf(ffn_w_in), ffn_w_out=bf(ffn_w_out),
    )
    yp, hg_p, ret_p, rw_p, sh_p, lru_p, conv_p = _trunk(x_prompt, 0, None, p)
    st = dict(hgrn=state_hgrn, ret=state_ret, rwkv=state_rwkv, shift=state_rwkv_shift,
              lru=state_lru, conv=state_lru_conv)
    ys, hg_s, ret_s, rw_s, sh_s, lru_s, conv_s = _trunk(x_sample, PAST_LEN, st, p)
    return (yp, ys, hg_p, hg_s, ret_p, ret_s, rw_p, rw_s, sh_p, sh_s, lru_p, lru_s, conv_p, conv_s)
```

```python
import functools
import math

import jax
import jax.numpy as jnp
from jax import lax
from jax.experimental import pallas as pl
from jax.experimental.pallas import tpu as pltpu

F32 = jnp.float32
BF16 = jnp.bfloat16

D_MODEL = 1024
NORM_EPS = 1e-6
HG_HEADS, HG_DK, HG_BLOCK = 8, 128, 16
HG_SUPER = 4
RET_HEADS, RET_DK, RET_DV = 4, 256, 512
ROPE_BASE = 10000.0
RW_HEADS, RW_HEAD = 16, 64
RW_GROUP = 4
RW_GN_EPS = 64e-5
LRU_BLOCKS, LRU_BW, CONV_W, LRU_C = 8, 128, 4, 8.0
D_FF = 2816
PAST_LEN = 2048

VMEM_LIMIT_BYTES = 56 * 1024 * 1024


def _cparams(*sem):
    return pltpu.CompilerParams(dimension_semantics=sem, vmem_limit_bytes=VMEM_LIMIT_BYTES)


def _const_spec(shape):
    n = len(shape)
    return pl.BlockSpec(tuple(shape), lambda *_: (0,) * n, pipeline_mode=pl.Buffered(1))


def _iota(shape, dim):
    return lax.broadcasted_iota(jnp.int32, shape, dim)


def _dot(a, b):
    return jnp.dot(a, b, preferred_element_type=F32)


def _dot_nt(a, b):
    return lax.dot_general(a, b, (((1,), (1,)), ((), ())), preferred_element_type=F32)


def _dot_tn(a, b):
    return lax.dot_general(a, b, (((0,), (0,)), ((), ())), preferred_element_type=F32)


def _split3(x):
    hi = x.astype(BF16)
    r1 = x - hi.astype(F32)
    mid = r1.astype(BF16)
    lo = (r1 - mid.astype(F32)).astype(BF16)
    return hi, mid, lo


def _sel_dot(m01, x, pieces=3):
    return sum(_dot(m01, p) for p in _split3(x)[:pieces])


def _dot_sel(x, m01, pieces=3):
    return sum(_dot(p, m01) for p in _split3(x)[:pieces])


def _rms(x, w):
    return x * lax.rsqrt(jnp.mean(x * x, axis=-1, keepdims=True) + NORM_EPS) * w


def _softplus(x):
    return jnp.maximum(x, 0.0) + jnp.log1p(jnp.exp(-jnp.abs(x)))


def _log2(n):
    l = int(math.log2(n))
    assert 1 << l == n, n
    return l


def _norm_mm_body(x_ref, nw_ref, w_ref, o_ref, *, tn):
    h = _rms(x_ref[...], nw_ref[...]).astype(BF16)
    for c in range(w_ref.shape[1] // tn):
        o_ref[:, c * tn:(c + 1) * tn] = _dot(h, w_ref[:, c * tn:(c + 1) * tn])


def norm_matmul(x2d, nw, w, *, tm=512, tn=1024):
    m, k = x2d.shape
    n = w.shape[1]
    tm, tn = min(tm, m), min(tn, n)
    return pl.pallas_call(
        functools.partial(_norm_mm_body, tn=tn),
        grid=(m // tm,),
        in_specs=[pl.BlockSpec((tm, k), lambda i: (i, 0)),
                  _const_spec((1, k)),
                  _const_spec((k, n))],
        out_specs=pl.BlockSpec((tm, n), lambda i: (i, 0)),
        out_shape=jax.ShapeDtypeStruct((m, n), F32),
        compiler_params=_cparams("parallel"),
        name="norm_matmul",
    )(x2d, nw.reshape(1, k), w)


def _mix_ffn_body(o_ref, wo_ref, x_ref, nw_ref, win_ref, wout_ref, fnw_ref, y_ref, *, final_norm):
    x1 = x_ref[...] + _dot(o_ref[...].astype(BF16), wo_ref[...])
    h = _rms(x1, nw_ref[...]).astype(BF16)
    g = _dot(h, win_ref[:, 0:D_FF])
    u = _dot(h, win_ref[:, D_FF:2 * D_FF])
    act = (g * jax.nn.sigmoid(g) * u).astype(BF16)
    acc = x1 + _dot(act, wout_ref[...])
    if final_norm:
        acc = _rms(acc, fnw_ref[...])
    y_ref[...] = acc


def mix_out_ffn(o2d, w_o, x2d, nw, w_in, w_out, fnw, *, final_norm, tm=512):
    m, ko = o2d.shape
    d = x2d.shape[1]
    tm = min(tm, m)
    return pl.pallas_call(
        functools.partial(_mix_ffn_body, final_norm=final_norm),
        grid=(m // tm,),
        in_specs=[pl.BlockSpec((tm, ko), lambda i: (i, 0)),
                  _const_spec((ko, d)),
                  pl.BlockSpec((tm, d), lambda i: (i, 0)),
                  _const_spec((1, d)),
                  _const_spec((d, 2 * D_FF)),
                  _const_spec((D_FF, d)),
                  _const_spec((1, d))],
        out_specs=pl.BlockSpec((tm, d), lambda i: (i, 0)),
        out_shape=jax.ShapeDtypeStruct((m, d), F32),
        compiler_params=_cparams("parallel"),
        name="mix_out_ffn",
    )(o2d, w_o, x2d, nw.reshape(1, d), w_in, w_out, fnw.reshape(1, d))


def _hgrn_body(*refs, tc, layer, has_state):
    if has_state:
        q_ref, f_ref, v_ref, g_ref, lbp_ref, gn_ref, s0_ref, o_ref, sout_ref, st_ref = refs
    else:
        q_ref, f_ref, v_ref, g_ref, lbp_ref, gn_ref, o_ref, sout_ref, st_ref = refs
    t = pl.program_id(1)
    blk = min(HG_BLOCK, tc)
    heads, dk = HG_HEADS, HG_DK
    nb = tc // blk
    m = min(HG_SUPER, nb)
    sbr = m * blk

    @pl.when(t == 0)
    def _():
        for h in range(heads):
            st_ref[h] = s0_ref[0, h].T if has_state else jnp.zeros((dk, dk), F32)

    lbp = lbp_ref[...]
    e = jnp.exp(lbp - jnp.max(lbp, axis=0, keepdims=True))
    lb = jnp.sum(e[:layer + 1], axis=0, keepdims=True) / jnp.sum(e, axis=0, keepdims=True)

    fl = f_ref[0]
    z = jnp.exp(-jnp.abs(fl))
    big = 1.0 / (1.0 + z)
    small = z * big
    pos = fl >= 0.0
    f = lb + (1.0 - lb) * jnp.where(pos, big, small)
    k = (1.0 - lb) * jnp.where(pos, small, big)
    lf = jnp.log(f)
    sh = _log2(blk)
    ri, ci = _iota((tc, tc), 0), _iota((tc, tc), 1)
    bdiff = (ri >> sh) - (ci >> sh)
    same_sb = (ri >> _log2(sbr)) == (ci >> _log2(sbr))
    causal = (bdiff == 0) & (ci <= ri)
    cum = _sel_dot(causal.astype(BF16), lf, pieces=2)
    tot = _sel_dot((bdiff == 0).astype(BF16), lf, pieces=2)
    q_in = q_ref[0] * jnp.exp(cum)
    k_in = k * jnp.exp(-cum)
    kst1 = k * jnp.exp(tot - cum)
    rows_of = lambda b: slice(b * blk, (b + 1) * blk)
    drow = [jnp.exp(tot[b * blk:b * blk + 1, :]) for b in range(nb)]
    kst = {1: [kst1[rows_of(b)] for b in range(nb)]}
    for dd in range(2, m + 1):
        kst[dd] = [kst[dd - 1][b] * drow[b + dd - 1] if (b % m) + dd - 1 <= m - 1 else None for b in range(nb)]
    zero_blk = jnp.zeros((blk, heads * dk), F32)
    k_far = [jnp.concatenate([x if x is not None else zero_blk for x in kst[dd]], axis=0) for dd in range(2, m)]
    k_end = jnp.concatenate([kst[m - (b % m)][b] for b in range(nb)], axis=0)
    q_sb = []
    for b in range(nb):
        x = q_in[rows_of(b)]
        for l in range(1, (b % m) + 1):
            x = x * drow[b - l]
        q_sb.append(x)
    q_sb = jnp.concatenate(q_sb, axis=0)
    dec_sb = []
    for sbi in range(nb // m):
        x = drow[sbi * m]
        for b in range(sbi * m + 1, (sbi + 1) * m):
            x = x * drow[b]
        dec_sb.append(x)

    v = v_ref[0]
    sls = [slice(h * dk, (h + 1) * dk) for h in range(heads)]
    rhs = [jnp.concatenate([k_in[:, sl], kst1[:, sl]] + [x[:, sl] for x in k_far], axis=0).astype(BF16) for sl in sls]
    sc = [_dot_nt(q_in[:, sl].astype(BF16), rhs[h]) for h, sl in enumerate(sls)]
    masks = [causal] + [same_sb & (bdiff == dd) for dd in range(1, m)]
    sc = [sum(jnp.where(masks[dd], x[:, dd * tc:(dd + 1) * tc], 0.0) for dd in range(m)).astype(BF16) for x in sc]
    o = [_dot(sc[h], v[:, sl].astype(BF16)) for h, sl in enumerate(sls)]
    st = [st_ref[h] for h in range(heads)]
    o_in = [[] for _ in range(heads)]
    for sbi in range(nb // m):
        rows = slice(sbi * sbr, (sbi + 1) * sbr)
        for h, sl in enumerate(sls):
            o_in[h].append(_dot_nt(q_sb[rows, sl].astype(BF16), st[h].astype(BF16)))
        kv = [_dot_tn(v[rows, sl].astype(BF16), k_end[rows, sl].astype(BF16)) for sl in sls]
        st = [st[h] * dec_sb[sbi][:, sl] + kv[h] for h, sl in enumerate(sls)]
    for h in range(heads):
        st_ref[h] = st[h]

    g = g_ref[0]
    gn = gn_ref[...]
    for h, sl in enumerate(sls):
        oh = o[h] + jnp.concatenate(o_in[h], axis=0)
        oh = oh * lax.rsqrt(jnp.mean(oh * oh, axis=-1, keepdims=True) + NORM_EPS)
        gh = g[:, sl]
        o_ref[0, :, sl] = (oh * gn[:, sl] * (gh * jax.nn.sigmoid(gh))).astype(BF16)

    @pl.when(t == pl.num_programs(1) - 1)
    def _():
        for h in range(heads):
            sout_ref[0, h] = st_ref[h].T


def hgrn_scan(proj, hg_lb, g_norm, s0, *, layer, tc=128):
    b, t, _ = proj.shape
    d = D_MODEL
    tc = min(tc, t)
    has_state = s0 is not None
    col = lambda c: pl.BlockSpec((1, tc, d), lambda i, j, c=c: (i, j, c))
    in_specs = [col(0), col(1), col(2), col(3),
                pl.BlockSpec(hg_lb.shape, lambda i, j: (0, 0)),
                pl.BlockSpec((1, d), lambda i, j: (0, 0))]
    args = [proj, proj, proj, proj, hg_lb, g_norm.reshape(1, d)]
    st_spec = pl.BlockSpec((1, HG_HEADS, HG_DK, HG_DK), lambda i, j: (i, 0, 0, 0))
    if has_state:
        in_specs.append(st_spec)
        args.append(s0)
    return pl.pallas_call(
        functools.partial(_hgrn_body, tc=tc, layer=layer, has_state=has_state),
        grid=(b, t // tc),
        in_specs=in_specs,
        out_specs=[pl.BlockSpec((1, tc, d), lambda i, j: (i, j, 0)), st_spec],
        out_shape=[jax.ShapeDtypeStruct((b, t, d), BF16),
                   jax.ShapeDtypeStruct((b, HG_HEADS, HG_DK, HG_DK), F32)],
        scratch_shapes=[pltpu.VMEM((HG_HEADS, HG_DK, HG_DK), F32)],
        compiler_params=_cparams("parallel", "arbitrary"),
        name="hgrn_scan",
    )(*args)


def _rope_body(cos_ref, sin_ref, *, pos0, tc):
    half = RET_DK // 2
    pos = (pos0 + pl.program_id(0) * tc + _iota((tc, half), 0)).astype(F32)
    inv = jnp.power(jnp.float32(ROPE_BASE), -(_iota((tc, half), 1).astype(F32) / half))
    ang = pos * inv
    cos_ref[...] = jnp.cos(ang)
    sin_ref[...] = jnp.sin(ang)


def rope_table(pos0, t, *, tc=256):
    tc = min(tc, t)
    half = RET_DK // 2
    return pl.pallas_call(
        functools.partial(_rope_body, pos0=pos0, tc=tc),
        grid=(t // tc,),
        out_specs=[pl.BlockSpec((tc, half), lambda i: (i, 0))] * 2,
        out_shape=[jax.ShapeDtypeStruct((t, half), F32)] * 2,
        compiler_params=_cparams("parallel"),
        name="rope_table",
    )()


def _ret_body(*refs, tc, has_state):
    if has_state:
        q_ref, k_ref, v_ref, g_ref, cos_ref, sin_ref, gn_ref, s0_ref, o_ref, sout_ref, s_ref = refs
    else:
        q_ref, k_ref, v_ref, g_ref, cos_ref, sin_ref, gn_ref, o_ref, sout_ref, s_ref = refs
    t = pl.program_id(1)
    nh, dk, dv = RET_HEADS, RET_DK, RET_DV
    half = dk // 2

    @pl.when(t == 0)
    def _():
        s_ref[...] = s0_ref[0] if has_state else jnp.zeros((nh, dk, dv), F32)

    cos, sin = cos_ref[...], sin_ref[...]

    def rot(x):
        x1, x2 = x[:, :half], x[:, half:]
        return jnp.concatenate([x1 * cos - x2 * sin, x1 * sin + x2 * cos], axis=-1)

    heads = range(nh)
    lg = [math.log1p(-2.0 ** (-5.0 - h)) for h in heads]
    ksl = [slice(h * dk, (h + 1) * dk) for h in heads]
    vsl = [slice(h * dv, (h + 1) * dv) for h in heads]
    diff = (_iota((tc, tc), 0) - _iota((tc, tc), 1)).astype(F32)
    row = _iota((tc, 1), 0).astype(F32)
    q = [rot(q_ref[0, :, sl]) for sl in ksl]
    k = [rot(k_ref[0, :, sl]) * (dk ** -0.5) for sl in ksl]
    v = [v_ref[0, :, sl].astype(BF16) for sl in vsl]
    s = [s_ref[h] for h in heads]
    sc = [_dot_nt(q[h].astype(BF16), k[h].astype(BF16)) for h in heads]
    sc = [(jnp.where(diff >= 0.0, jnp.exp(diff * lg[h]), 0.0) * sc[h]).astype(BF16) for h in heads]
    o = [_dot(sc[h], v[h]) + _dot((q[h] * jnp.exp((row + 1.0) * lg[h])).astype(BF16), s[h].astype(BF16))
         for h in heads]
    kv = [_dot_tn((k[h] * jnp.exp((tc - 1.0 - row) * lg[h])).astype(BF16), v[h]) for h in heads]
    for h in heads:
        s_ref[h] = s[h] * math.exp(tc * lg[h]) + kv[h]
        oh = o[h] * lax.rsqrt(jnp.mean(o[h] * o[h], axis=-1, keepdims=True) + NORM_EPS)
        g = g_ref[0, :, vsl[h]]
        o_ref[0, :, vsl[h]] = (oh * gn_ref[:, vsl[h]] * (g * jax.nn.sigmoid(g))).astype(BF16)

    @pl.when(t == pl.num_programs(1) - 1)
    def _():
        sout_ref[0] = s_ref[...]


def ret_scan(proj, cos, sin, g_norm, s0, *, tc=256):
    b, t, _ = proj.shape
    tc = min(tc, t)
    has_state = s0 is not None
    nh, dk, dv = RET_HEADS, RET_DK, RET_DV
    half = dk // 2
    in_specs = [pl.BlockSpec((1, tc, nh * dk), lambda i, j: (i, j, 0)),
                pl.BlockSpec((1, tc, nh * dk), lambda i, j: (i, j, 1)),
                pl.BlockSpec((1, tc, nh * dv), lambda i, j: (i, j, 1)),
                pl.BlockSpec((1, tc, nh * dv), lambda i, j: (i, j, 2)),
                pl.BlockSpec((tc, half), lambda i, j: (j, 0)),
                pl.BlockSpec((tc, half), lambda i, j: (j, 0)),
                _const_spec((1, nh * dv))]
    args = [proj, proj, proj, proj, cos, sin, g_norm.reshape(1, nh * dv)]
    st_spec = pl.BlockSpec((1, nh, dk, dv), lambda i, j: (i, 0, 0, 0))
    if has_state:
        in_specs.append(st_spec)
        args.append(s0)
    return pl.pallas_call(
        functools.partial(_ret_body, tc=tc, has_state=has_state),
        grid=(b, t // tc),
        in_specs=in_specs,
        out_specs=[pl.BlockSpec((1, tc, nh * dv), lambda i, j: (i, j, 0)), st_spec],
        out_shape=[jax.ShapeDtypeStruct((b, t, nh * dv), BF16),
                   jax.ShapeDtypeStruct((b, nh, dk, dv), F32)],
        scratch_shapes=[pltpu.VMEM((nh, dk, dv), F32)],
        compiler_params=_cparams("parallel", "arbitrary"),
        name="ret_scan",
    )(*args)


def _rwkv_proj_body(*refs, has_state):
    if has_state:
        (x_ref, nw_ref, sh0_ref, mu_ref, wrkv_ref, w0_ref, w1_ref, w2_ref, a0_ref, a1_ref, a2_ref,
         g1_ref, g2_ref, kk_ref, ka_ref, hsum_ref, hexp_ref,
         r_o, k_o, v_o, kk_o, a_o, lw_o, g_o, sh_o, carry_ref) = refs
    else:
        (x_ref, nw_ref, mu_ref, wrkv_ref, w0_ref, w1_ref, w2_ref, a0_ref, a1_ref, a2_ref,
         g1_ref, g2_ref, kk_ref, ka_ref, hsum_ref, hexp_ref,
         r_o, k_o, v_o, kk_o, a_o, lw_o, g_o, sh_o, carry_ref) = refs
    t = pl.program_id(1)
    tm = x_ref.shape[1]

    @pl.when(t == 0)
    def _():
        carry_ref[...] = sh0_ref[0] if has_state else jnp.zeros((1, D_MODEL), F32)

    h = _rms(x_ref[0], nw_ref[...])
    prev = jnp.where(_iota((tm, 1), 0) == 0, carry_ref[...], pltpu.roll(h, 1, axis=0))
    last = h[tm - 1:tm, :]
    carry_ref[...] = last
    sh_o[0] = last
    d = prev - h
    mix = lambda i: (h + d * mu_ref[i:i + 1, :]).astype(BF16)
    r = _dot(mix(0), wrkv_ref[0])
    k = _dot(mix(1), wrkv_ref[1])
    v = _dot(mix(2), wrkv_ref[2])
    w_pre = w0_ref[...] + _dot(jnp.tanh(_dot(mix(3), w1_ref[...])).astype(BF16), w2_ref[...])
    lw_o[0] = -jnp.exp(-_softplus(-w_pre) - 0.5)
    a = jax.nn.sigmoid(a0_ref[...] + _dot(_dot(mix(4), a1_ref[...]).astype(BF16), a2_ref[...]))
    g_o[0] = _dot(jax.nn.sigmoid(_dot(mix(5), g1_ref[...])).astype(BF16), g2_ref[...])
    kk = k * kk_ref[...]
    ss = _dot_sel(kk * kk, hsum_ref[...], pieces=2)
    inv = 1.0 / jnp.maximum(jnp.sqrt(ss), 1e-12)
    kk_o[0] = kk * _dot_sel(inv, hexp_ref[...], pieces=2)
    r_o[0] = r
    k_o[0] = k * (1.0 + (a - 1.0) * ka_ref[...])
    v_o[0] = v
    a_o[0] = a


def rwkv_proj(x, nw, shift0, p, *, tm=256):
    b, t, d = x.shape
    tm = min(tm, t)
    has_state = shift0 is not None
    row = lambda a: a.reshape(1, d)
    head_of_lane = jnp.arange(d) // RW_HEAD
    hsum = (head_of_lane[:, None] == jnp.arange(RW_HEADS)[None, :]).astype(BF16)
    hexp = hsum.T
    full = lambda a: pl.BlockSpec(a.shape, lambda i, j, n=a.ndim: (0,) * n)
    tok = pl.BlockSpec((1, tm, d), lambda i, j: (i, j, 0))
    args = [x, row(nw)]
    in_specs = [tok, full(row(nw))]
    if has_state:
        args.append(shift0.reshape(b, 1, d))
        in_specs.append(pl.BlockSpec((1, 1, d), lambda i, j: (i, 0, 0)))
    consts = [p['mu'], p['w_rkv'], row(p['w0']), p['w1'], p['w2'], row(p['a0']), p['a1'], p['a2'],
              p['g1'], p['g2'], row(p['k_k']), row(p['k_a']), hsum, hexp]
    args += consts
    in_specs += [full(a) for a in consts]
    outs = pl.pallas_call(
        functools.partial(_rwkv_proj_body, has_state=has_state),
        grid=(b, t // tm),
        in_specs=in_specs,
        out_specs=[tok] * 7 + [pl.BlockSpec((1, 1, d), lambda i, j: (i, 0, 0))],
        out_shape=[jax.ShapeDtypeStruct((b, t, d), F32)] * 7 + [jax.ShapeDtypeStruct((b, 1, d), F32)],
        scratch_shapes=[pltpu.VMEM((1, d), F32)],
        compiler_params=_cparams("parallel", "arbitrary"),
        name="rwkv_proj",
    )(*args)
    return outs[:7], outs[7].reshape(b, d)


def _rwkv_scan_body(*refs, L, has_state):
    if has_state:
        (r_ref, k_ref, v_ref, kk_ref, a_ref, lw_ref, g_ref, rk_ref, lnw_ref, lnb_ref, s0_ref,
         o_ref, sout_ref, s_ref) = refs
    else:
        (r_ref, k_ref, v_ref, kk_ref, a_ref, lw_ref, g_ref, rk_ref, lnw_ref, lnb_ref,
         o_ref, sout_ref, s_ref) = refs
    t = pl.program_id(1)
    gw = RW_GROUP * RW_HEAD
    ngroups = RW_HEADS // RW_GROUP
    sl_l, sl_h = _log2(L), _log2(RW_HEAD)
    n = RW_GROUP * L

    @pl.when(t == 0)
    def _():
        s_ref[...] = s0_ref[...] if has_state else jnp.zeros(s_ref.shape, F32)

    nrows = lw_ref.shape[0]
    tri = (_iota((L, L), 1) <= _iota((L, L), 0)).astype(BF16)
    v_r, a_hat_r, r_hat_r, b_hat_r, k_hat_r, b_til_r, k_til_r, gam_l_r, bonus_r = ([] for _ in range(9))
    for rb in range(nrows):
        lw = lw_ref[rb]
        cum = _sel_dot(tri, lw)
        cum_l = cum[L - 1:L, :]
        e_pos, e_neg = jnp.exp(cum), jnp.exp(-cum)
        e_tail = jnp.exp(cum_l - cum)
        kk, a, r, k = kk_ref[rb], a_ref[rb], r_ref[rb], k_ref[rb]
        beta = kk * a
        v_r.append(v_ref[rb])
        a_hat_r.append(-kk * jnp.exp(cum - lw))
        r_hat_r.append(r * e_pos)
        b_hat_r.append(beta * e_neg)
        k_hat_r.append(k * e_neg)
        b_til_r.append(beta * e_tail)
        k_til_r.append(k * e_tail)
        gam_l_r.append(jnp.exp(cum_l))
        bonus_r.append(r * k * rk_ref[...])

    own = (_iota((n, 1), 0) >> sl_l) == (_iota((1, gw), 1) >> sl_h)
    tile = lambda x: jnp.concatenate([x] * RW_GROUP, axis=0)
    stack = lambda x: jnp.where(own, tile(x), 0.0).astype(BF16)
    same = (_iota((n, n), 0) >> sl_l) == (_iota((n, n), 1) >> sl_l)
    to_bd = lambda x: jnp.where(same, tile(x), 0.0).astype(BF16)
    row_c, col_c = _iota((L, n), 0), _iota((L, n), 1) & (L - 1)
    strict_c, incl_c = col_c < row_c, col_c <= row_c
    eye_c = (col_c == row_c).astype(F32)
    bdg = (_iota((gw, gw), 0) >> sl_h) == (_iota((gw, gw), 1) >> sl_h)
    bdg16 = bdg.astype(BF16)
    inv_n = 1.0 / RW_HEAD

    units = [(rb, gi) for rb in range(nrows) for gi in range(ngroups)]
    groups = range(len(units))
    sls = [slice(gi * gw, (gi + 1) * gw) for _, gi in units]
    pick = lambda per_row: [per_row[rb][:, sl] for (rb, _), sl in zip(units, sls)]
    v_u, b_til_u, k_til_u, bonus_u = pick(v_r), pick(b_til_r), pick(k_til_r), pick(bonus_r)
    gam_l_u = pick(gam_l_r)
    ar = [jnp.concatenate([x, y], axis=0).astype(BF16) for x, y in zip(pick(a_hat_r), pick(r_hat_r))]
    c = [_dot_nt(ar[u], jnp.concatenate([stack(x), stack(y)], axis=0))
         for u, (x, y) in enumerate(zip(pick(b_hat_r), pick(k_hat_r)))]
    p_c = [jnp.where(strict_c, x[:L, :n], 0.0) for x in c]
    m_k = [jnp.concatenate([jnp.where(strict_c, x[:L, n:], 0.0), jnp.where(incl_c, x[L:, n:], 0.0)],
                           axis=0).astype(BF16) for x in c]
    m_rb = [jnp.where(incl_c, x[L:, :n], 0.0).astype(BF16) for x in c]
    t_c = [eye_c + x for x in p_c]
    for j in range(sl_l):
        p_bd = [to_bd(x) for x in p_c]
        if j == 0:
            p_c = [_dot(p_c[gi].astype(BF16), p_bd[gi]) for gi in groups]
        elif j < sl_l - 1:
            both = [_dot(jnp.concatenate([p_c[gi], t_c[gi]], axis=0).astype(BF16), p_bd[gi]) for gi in groups]
            p_c = [x[:L] for x in both]
            t_c = [t_c[gi] + both[gi][L:] for gi in groups]
        else:
            t_c = [t_c[gi] + _dot(t_c[gi].astype(BF16), p_bd[gi]) for gi in groups]
    s_g = [s_ref[rb, gi] for rb, gi in units]
    q = [_dot_nt(ar[u], s_g[u].astype(BF16)) for u in groups]
    kv = [_dot(m_k[u], stack(v_u[u])) for u in groups]
    u_c = [_dot(t_c[u].astype(BF16), stack(q[u][:L] + kv[u][:L])) for u in groups]
    y = [q[u][L:] + kv[u][L:] + _dot(m_rb[u], stack(u_c[u])) for u in groups]
    for u, (rb, gi) in enumerate(units):
        upd = _dot_tn(jnp.concatenate([u_c[u], v_u[u]], axis=0).astype(BF16),
                      jnp.concatenate([b_til_u[u], k_til_u[u]], axis=0).astype(BF16))
        s_ref[rb, gi] = s_g[u] * gam_l_u[u] + jnp.where(bdg, upd, 0.0)
    pieces = []
    for u in groups:
        pieces += _split3(y[u]) + _split3(y[u] * y[u]) + _split3(bonus_u[u])
    sums = _dot(jnp.concatenate(pieces, axis=0), bdg16)
    for u, (rb, gi) in enumerate(units):
        sl = sls[u]
        sum3 = lambda i, base=9 * u * L: sums[base + 3 * i * L:base + (3 * i + 1) * L] \
            + sums[base + (3 * i + 1) * L:base + (3 * i + 2) * L] + sums[base + (3 * i + 2) * L:base + (3 * i + 3) * L]
        mu = sum3(0) * inv_n
        var = sum3(1) * inv_n - mu * mu
        yn = (y[u] - mu) * lax.rsqrt(var + RW_GN_EPS) * lnw_ref[:, sl] + lnb_ref[:, sl]
        o_ref[rb, :, sl] = ((yn + sum3(2) * v_u[u]) * g_ref[rb, :, sl]).astype(BF16)

    @pl.when(t == pl.num_programs(1) - 1)
    def _():
        sout_ref[...] = s_ref[...]


def rwkv_scan(streams, r_k, ln_w, ln_b, s0_bd, *, chunk=64, rows=2):
    b, t, d = streams[0].shape
    L = min(chunk, t)
    nrows = rows if b % rows == 0 else 1
    has_state = s0_bd is not None
    ngroups = RW_HEADS // RW_GROUP
    gw = RW_GROUP * RW_HEAD
    tok = pl.BlockSpec((nrows, L, d), lambda i, j: (i, j, 0))
    vec = pl.BlockSpec((1, d), lambda i, j: (0, 0))
    st_spec = pl.BlockSpec((nrows, ngroups, gw, gw), lambda i, j: (i, 0, 0, 0))
    args = list(streams) + [r_k.reshape(1, d), ln_w.reshape(1, d), ln_b.reshape(1, d)]
    in_specs = [tok] * 7 + [vec] * 3
    if has_state:
        args.append(s0_bd)
        in_specs.append(st_spec)
    return pl.pallas_call(
        functools.partial(_rwkv_scan_body, L=L, has_state=has_state),
        grid=(b // nrows, t // L),
        in_specs=in_specs,
        out_specs=[tok, st_spec],
        out_shape=[jax.ShapeDtypeStruct((b, t, d), BF16),
                   jax.ShapeDtypeStruct((b, ngroups, gw, gw), F32)],
        scratch_shapes=[pltpu.VMEM((nrows, ngroups, gw, gw), F32)],
        compiler_params=_cparams("parallel", "arbitrary"),
        name="rwkv_scan",
    )(*args)


def _rwkv_state_to_bd(s):
    b = s.shape[0]
    g, m, hd = RW_HEADS // RW_GROUP, RW_GROUP, RW_HEAD
    s = s.reshape(b, g, m, hd, 1, hd) * jnp.eye(m, dtype=s.dtype)[None, None, :, None, :, None]
    return s.reshape(b, g, m * hd, m * hd)


def _rwkv_state_from_bd(sbd):
    b = sbd.shape[0]
    g, m, hd = RW_HEADS // RW_GROUP, RW_GROUP, RW_HEAD
    s = sbd.reshape(b, g, m, hd, m, hd)
    s = jnp.stack([s[:, :, i, :, i, :] for i in range(m)], axis=2)
    return s.reshape(b, RW_HEADS, hd, hd)


def _lru_body(*refs, tc, has_state):
    if has_state:
        (gate_ref, xb_ref, cw_ref, cb_ref, wg_ref, bg_ref, lam_ref, h0_ref, c0_ref,
         y_ref, hout_ref, xpad_ref, hc_ref) = refs
    else:
        (gate_ref, xb_ref, cw_ref, cb_ref, wg_ref, bg_ref, lam_ref,
         y_ref, hout_ref, xpad_ref, hc_ref) = refs
    t = pl.program_id(1)
    d = D_MODEL
    pad = 8

    @pl.when(t == 0)
    def _():
        xpad_ref[0:pad, :] = c0_ref[0] if has_state else jnp.zeros((pad, d), F32)
        hc_ref[...] = h0_ref[0] if has_state else jnp.zeros((1, d), F32)

    xpad_ref[pad:pad + tc, :] = xb_ref[0]
    xc = cb_ref[...]
    for j in range(CONV_W):
        off = pad - (CONV_W - 1) + j
        xc = xc + xpad_ref[off:off + tc, :] * cw_ref[j:j + 1, :]
    xpad_ref[0:pad, :] = xpad_ref[tc:tc + pad, :]

    xcb = xc.astype(BF16)
    r_pre, i_pre = [], []
    for nb in range(LRU_BLOCKS):
        sl = slice(nb * LRU_BW, (nb + 1) * LRU_BW)
        r_pre.append(_dot(xcb[:, sl], wg_ref[0, nb]))
        i_pre.append(_dot(xcb[:, sl], wg_ref[1, nb]))
    r_gate = jax.nn.sigmoid(jnp.concatenate(r_pre, axis=-1) + bg_ref[0:1, :])
    i_gate = jax.nn.sigmoid(jnp.concatenate(i_pre, axis=-1) + bg_ref[1:2, :])
    log_a = -LRU_C * r_gate * _softplus(-lam_ref[...])
    a = jnp.exp(log_a)
    bv = jnp.sqrt(-jnp.tanh(log_a) * (a * a + 1.0)) * (i_gate * xc)

    grp = 8
    a = a.reshape(tc // grp, grp, d)
    bv = bv.reshape(tc // grp, grp, d)
    sub = _iota((1, grp, 1), 1)
    s = 1
    while s < grp:
        keep = sub >= s
        a_sh = jnp.where(keep, pltpu.roll(a, s, axis=1), 1.0)
        b_sh = jnp.where(keep, pltpu.roll(bv, s, axis=1), 0.0)
        bv = bv + a * b_sh
        a = a * a_sh
        s *= 2
    carry = hc_ref[...]
    rows = []
    for gi in range(tc // grp):
        h_g = bv[gi] + a[gi] * carry
        carry = h_g[grp - 1:grp, :]
        rows.append(h_g)
    hs = jnp.concatenate(rows, axis=0)
    last = carry
    hc_ref[...] = last
    hout_ref[0] = last
    y_ref[0] = (hs * jax.nn.gelu(gate_ref[0], approximate=True)).astype(BF16)


def lru_scan(proj, conv_w, conv_b, w_gates, b_gates, lam, h0, conv0, *, tc=256):
    b, t, _ = proj.shape
    d = D_MODEL
    tc = min(tc, t)
    has_state = h0 is not None
    full = lambda a: pl.BlockSpec(a.shape, lambda i, j, n=a.ndim: (0,) * n)
    consts = [conv_w, conv_b.reshape(1, d), w_gates, b_gates, lam.reshape(1, d)]
    args = [proj, proj] + consts
    in_specs = [pl.BlockSpec((1, tc, d), lambda i, j: (i, j, 0)),
                pl.BlockSpec((1, tc, d), lambda i, j: (i, j, 1))] + [full(a) for a in consts]
    if has_state:
        c0 = jnp.pad(conv0, ((0, 0), (8 - (CONV_W - 1), 0), (0, 0)))
        args += [h0.reshape(b, 1, d), c0]
        in_specs += [pl.BlockSpec((1, 1, d), lambda i, j: (i, 0, 0)),
                     pl.BlockSpec((1, 8, d), lambda i, j: (i, 0, 0))]
    y, hl = pl.pallas_call(
        functools.partial(_lru_body, tc=tc, has_state=has_state),
        grid=(b, t // tc),
        in_specs=in_specs,
        out_specs=[pl.BlockSpec((1, tc, d), lambda i, j: (i, j, 0)),
                   pl.BlockSpec((1, 1, d), lambda i, j: (i, 0, 0))],
        out_shape=[jax.ShapeDtypeStruct((b, t, d), BF16), jax.ShapeDtypeStruct((b, 1, d), F32)],
        scratch_shapes=[pltpu.VMEM((tc + 8, d), F32), pltpu.VMEM((1, d), F32)],
        compiler_params=_cparams("parallel", "arbitrary"),
        name="lru_scan",
    )(*args)
    return y, hl.reshape(b, d)


def _trunk(x, pos0, st, p):
    b, t, d = x.shape
    m = b * t
    get = (lambda name: st[name][0]) if st is not None else (lambda name: None)
    x2 = x.reshape(m, d)

    proj = norm_matmul(x2, p['norm_mix'][0], p['hg_w_in']).reshape(b, t, -1)
    o, hg_s = hgrn_scan(proj, p['hg_lb'], p['hg_norm'], get('hgrn'), layer=0)
    x2 = mix_out_ffn(o.reshape(m, d), p['hg_w_out'], x2, p['norm_ffn'][0], p['ffn_w_in'][0], p['ffn_w_out'][0],
                     p['norm_final'], final_norm=False)

    proj = norm_matmul(x2, p['norm_mix'][1], p['ret_w_in']).reshape(b, t, -1)
    cos, sin = rope_table(pos0, t)
    o, ret_s = ret_scan(proj, cos, sin, p['ret_norm'], get('ret'))
    x2 = mix_out_ffn(o.reshape(m, -1), p['ret_w_out'], x2, p['norm_ffn'][1], p['ffn_w_in'][1], p['ffn_w_out'][1],
                     p['norm_final'], final_norm=False)

    s0 = get('rwkv')
    streams, shift = rwkv_proj(x2.reshape(b, t, d), p['norm_mix'][2], get('shift'), p['rw'])
    o, rw_bd = rwkv_scan(streams, p['rw']['r_k'], p['rw']['ln_w'], p['rw']['ln_b'],
                         None if s0 is None else _rwkv_state_to_bd(s0))
    rw_s = _rwkv_state_from_bd(rw_bd)
    x2 = mix_out_ffn(o.reshape(m, d), p['rw']['w_out'], x2, p['norm_ffn'][2], p['ffn_w_in'][2], p['ffn_w_out'][2],
                     p['norm_final'], final_norm=False)

    proj = norm_matmul(x2, p['norm_mix'][3], p['lru_w_in']).reshape(b, t, -1)
    o, lru_s = lru_scan(proj, p['lru_conv_w'], p['lru_conv_b'], p['lru_w_gates'], p['lru_b_gates'],
                        p['lru_lambda'], get('lru'), get('conv'))
    if st is None:
        conv_s = proj[:, t - (CONV_W - 1):, d:]
    else:
        conv_s = jnp.concatenate([st['conv'][0], proj[:, :, d:]], axis=1)[:, -(CONV_W - 1):]
    y = mix_out_ffn(o.reshape(m, d), p['lru_w_out'], x2, p['norm_ffn'][3], p['ffn_w_in'][3], p['ffn_w_out'][3],
                    p['norm_final'], final_norm=True)
    return (y.reshape(b, t, d), hg_s[None], ret_s[None], rw_s[None], shift[None], lru_s[None], conv_s[None])


def kernel(x_prompt, x_sample, state_hgrn, state_ret, state_rwkv, state_rwkv_shift, state_lru, state_lru_conv, norm_mix, norm_ffn, norm_final, hg_lb, hg_w_in, hg_norm, hg_w_out, ret_w_in, ret_norm, ret_w_out, rw_mu, rw_w_rkv, rw_w0, rw_w1, rw_w2, rw_a0, rw_a1, rw_a2, rw_g1, rw_g2, rw_k_k, rw_k_a, rw_r_k, rw_ln_w, rw_ln_b, rw_w_out, lru_w_in, lru_conv_w, lru_conv_b, lru_w_gates, lru_b_gates, lru_lambda, lru_w_out, ffn_w_in, ffn_w_out):
    bf = lambda a: a.astype(BF16)
    p = dict(
        norm_mix=norm_mix, norm_ffn=norm_ffn, norm_final=norm_final,
        hg_lb=hg_lb, hg_w_in=bf(hg_w_in[0]), hg_norm=hg_norm[0], hg_w_out=bf(hg_w_out[0]),
        ret_w_in=bf(ret_w_in[0]), ret_norm=ret_norm[0], ret_w_out=bf(ret_w_out[0]),
        rw=dict(mu=rw_mu[0], w_rkv=bf(rw_w_rkv[0]), w0=rw_w0[0], w1=bf(rw_w1[0]), w2=bf(rw_w2[0]),
                a0=rw_a0[0], a1=bf(rw_a1[0]), a2=bf(rw_a2[0]), g1=bf(rw_g1[0]), g2=bf(rw_g2[0]),
                k_k=rw_k_k[0], k_a=rw_k_a[0], r_k=rw_r_k[0], ln_w=rw_ln_w[0], ln_b=rw_ln_b[0],
                w_out=bf(rw_w_out[0])),
        lru_w_in=bf(lru_w_in[0]), lru_conv_w=lru_conv_w[0], lru_conv_b=lru_conv_b[0],
        lru_w_gates=bf(lru_w_gates[0]), lru_b_gates=lru_b_gates[0], lru_lambda=lru_lambda[0],
        lru_w_out=bf(lru_w_out[0]),
        ffn_w_in=bf(ffn_w_in), ffn_w_out=bf(ffn_w_out),
    )
    yp, hg_p, ret_p, rw_p, sh_p, lru_p, conv_p = _trunk(x_prompt, 0, None, p)
    st = dict(hgrn=state_hgrn, ret=state_ret, rwkv=state_rwkv, shift=state_rwkv_shift,
              lru=state_lru, conv=state_lru_conv)
    ys, hg_s, ret_s, rw_s, sh_s, lru_s, conv_s = _trunk(x_sample, PAST_LEN, st, p)
    return (yp, ys, hg_p, hg_s, ret_p, ret_s, rw_p, rw_s, sh_p, sh_s, lru_p, lru_s, conv_p, conv_s)
```

```python
import functools
import math

import jax
import jax.numpy as jnp
from jax import lax
from jax.experimental import pallas as pl
from jax.experimental.pallas import tpu as pltpu

F32 = jnp.float32
BF16 = jnp.bfloat16

D_MODEL = 1024
NORM_EPS = 1e-6
HG_HEADS, HG_DK, HG_BLOCK = 8, 128, 16
HG_SUPER = 4
RET_HEADS, RET_DK, RET_DV = 4, 256, 512
ROPE_BASE = 10000.0
RW_HEADS, RW_HEAD = 16, 64
RW_GROUP = 4
RW_GN_EPS = 64e-5
LRU_BLOCKS, LRU_BW, CONV_W, LRU_C = 8, 128, 4, 8.0
D_FF = 2816
PAST_LEN = 2048

VMEM_LIMIT_BYTES = 56 * 1024 * 1024


def _cparams(*sem):
    return pltpu.CompilerParams(dimension_semantics=sem, vmem_limit_bytes=VMEM_LIMIT_BYTES)


def _const_spec(shape):
    n = len(shape)
    return pl.BlockSpec(tuple(shape), lambda *_: (0,) * n, pipeline_mode=pl.Buffered(1))


def _iota(shape, dim):
    return lax.broadcasted_iota(jnp.int32, shape, dim)


def _dot(a, b):
    return jnp.dot(a, b, preferred_element_type=F32)


def _dot_nt(a, b):
    return lax.dot_general(a, b, (((1,), (1,)), ((), ())), preferred_element_type=F32)


def _dot_tn(a, b):
    return lax.dot_general(a, b, (((0,), (0,)), ((), ())), preferred_element_type=F32)


def _split3(x):
    hi = x.astype(BF16)
    r1 = x - hi.astype(F32)
    mid = r1.astype(BF16)
    lo = (r1 - mid.astype(F32)).astype(BF16)
    return hi, mid, lo


def _sel_dot(m01, x, pieces=3):
    return sum(_dot(m01, p) for p in _split3(x)[:pieces])


def _dot_sel(x, m01, pieces=3):
    return sum(_dot(p, m01) for p in _split3(x)[:pieces])


def _rms(x, w):
    return x * lax.rsqrt(jnp.mean(x * x, axis=-1, keepdims=True) + NORM_EPS) * w


def _sigmoid(x):
    return 0.5 * jnp.tanh(0.5 * x) + 0.5


def _softplus(x):
    return jnp.maximum(x, 0.0) + jnp.log1p(jnp.exp(-jnp.abs(x)))


def _log2(n):
    l = int(math.log2(n))
    assert 1 << l == n, n
    return l


def _norm_mm_body(x_ref, nw_ref, w_ref, o_ref, *, tn):
    h = _rms(x_ref[...], nw_ref[...]).astype(BF16)
    for c in range(w_ref.shape[1] // tn):
        o_ref[:, c * tn:(c + 1) * tn] = _dot(h, w_ref[:, c * tn:(c + 1) * tn])


def norm_matmul(x2d, nw, w, *, tm=512, tn=1024):
    m, k = x2d.shape
    n = w.shape[1]
    tm, tn = min(tm, m), min(tn, n)
    return pl.pallas_call(
        functools.partial(_norm_mm_body, tn=tn),
        grid=(m // tm,),
        in_specs=[pl.BlockSpec((tm, k), lambda i: (i, 0)),
                  _const_spec((1, k)),
                  _const_spec((k, n))],
        out_specs=pl.BlockSpec((tm, n), lambda i: (i, 0)),
        out_shape=jax.ShapeDtypeStruct((m, n), F32),
        compiler_params=_cparams("parallel"),
        name="norm_matmul",
    )(x2d, nw.reshape(1, k), w)


def _mix_ffn_body(o_ref, wo_ref, x_ref, nw_ref, win_ref, wout_ref, fnw_ref, y_ref, *, final_norm):
    x1 = x_ref[...] + _dot(o_ref[...].astype(BF16), wo_ref[...])
    h = _rms(x1, nw_ref[...]).astype(BF16)
    g = _dot(h, win_ref[:, 0:D_FF])
    u = _dot(h, win_ref[:, D_FF:2 * D_FF])
    act = (g * _sigmoid(g) * u).astype(BF16)
    acc = x1 + _dot(act, wout_ref[...])
    if final_norm:
        acc = _rms(acc, fnw_ref[...])
    y_ref[...] = acc


def mix_out_ffn(o2d, w_o, x2d, nw, w_in, w_out, fnw, *, final_norm, tm=512):
    m, ko = o2d.shape
    d = x2d.shape[1]
    tm = min(tm, m)
    return pl.pallas_call(
        functools.partial(_mix_ffn_body, final_norm=final_norm),
        grid=(m // tm,),
        in_specs=[pl.BlockSpec((tm, ko), lambda i: (i, 0)),
                  _const_spec((ko, d)),
                  pl.BlockSpec((tm, d), lambda i: (i, 0)),
                  _const_spec((1, d)),
                  _const_spec((d, 2 * D_FF)),
                  _const_spec((D_FF, d)),
                  _const_spec((1, d))],
        out_specs=pl.BlockSpec((tm, d), lambda i: (i, 0)),
        out_shape=jax.ShapeDtypeStruct((m, d), F32),
        compiler_params=_cparams("parallel"),
        name="mix_out_ffn",
    )(o2d, w_o, x2d, nw.reshape(1, d), w_in, w_out, fnw.reshape(1, d))


def _hgrn_body(*refs, tc, layer, has_state):
    if has_state:
        q_ref, f_ref, v_ref, g_ref, lbp_ref, gn_ref, s0_ref, o_ref, sout_ref, st_ref = refs
    else:
        q_ref, f_ref, v_ref, g_ref, lbp_ref, gn_ref, o_ref, sout_ref, st_ref = refs
    t = pl.program_id(1)
    blk = min(HG_BLOCK, tc)
    heads, dk = HG_HEADS, HG_DK
    nb = tc // blk
    m = min(HG_SUPER, nb)
    sbr = m * blk

    @pl.when(t == 0)
    def _():
        for h in range(heads):
            st_ref[h] = s0_ref[0, h].T if has_state else jnp.zeros((dk, dk), F32)

    lbp = lbp_ref[...]
    e = jnp.exp(lbp - jnp.max(lbp, axis=0, keepdims=True))
    lb = jnp.sum(e[:layer + 1], axis=0, keepdims=True) / jnp.sum(e, axis=0, keepdims=True)

    fl = f_ref[0]
    z = jnp.exp(-jnp.abs(fl))
    big = 1.0 / (1.0 + z)
    small = z * big
    pos = fl >= 0.0
    f = lb + (1.0 - lb) * jnp.where(pos, big, small)
    k = (1.0 - lb) * jnp.where(pos, small, big)
    lf = jnp.log(f)
    sh = _log2(blk)
    ri, ci = _iota((tc, tc), 0), _iota((tc, tc), 1)
    bdiff = (ri >> sh) - (ci >> sh)
    same_sb = (ri >> _log2(sbr)) == (ci >> _log2(sbr))
    causal = (bdiff == 0) & (ci <= ri)
    cum = _sel_dot(causal.astype(BF16), lf, pieces=2)
    tot = _sel_dot((bdiff == 0).astype(BF16), lf, pieces=2)
    q_in = q_ref[0] * jnp.exp(cum)
    k_in = k * jnp.exp(-cum)
    kst1 = k * jnp.exp(tot - cum)
    rows_of = lambda b: slice(b * blk, (b + 1) * blk)
    drow = [jnp.exp(tot[b * blk:b * blk + 1, :]) for b in range(nb)]
    kst = {1: [kst1[rows_of(b)] for b in range(nb)]}
    for dd in range(2, m + 1):
        kst[dd] = [kst[dd - 1][b] * drow[b + dd - 1] if (b % m) + dd - 1 <= m - 1 else None for b in range(nb)]
    zero_blk = jnp.zeros((blk, heads * dk), F32)
    k_far = [jnp.concatenate([x if x is not None else zero_blk for x in kst[dd]], axis=0) for dd in range(2, m)]
    k_end = jnp.concatenate([kst[m - (b % m)][b] for b in range(nb)], axis=0)
    q_sb = []
    for b in range(nb):
        x = q_in[rows_of(b)]
        for l in range(1, (b % m) + 1):
            x = x * drow[b - l]
        q_sb.append(x)
    q_sb = jnp.concatenate(q_sb, axis=0)
    dec_sb = []
    for sbi in range(nb // m):
        x = drow[sbi * m]
        for b in range(sbi * m + 1, (sbi + 1) * m):
            x = x * drow[b]
        dec_sb.append(x)

    v = v_ref[0]
    sls = [slice(h * dk, (h + 1) * dk) for h in range(heads)]
    rhs = [jnp.concatenate([k_in[:, sl], kst1[:, sl]] + [x[:, sl] for x in k_far], axis=0).astype(BF16) for sl in sls]
    sc = [_dot_nt(q_in[:, sl].astype(BF16), rhs[h]) for h, sl in enumerate(sls)]
    masks = [causal] + [same_sb & (bdiff == dd) for dd in range(1, m)]
    sc = [sum(jnp.where(masks[dd], x[:, dd * tc:(dd + 1) * tc], 0.0) for dd in range(m)).astype(BF16) for x in sc]
    o = [_dot(sc[h], v[:, sl].astype(BF16)) for h, sl in enumerate(sls)]
    st = [st_ref[h] for h in range(heads)]
    o_in = [[] for _ in range(heads)]
    for sbi in range(nb // m):
        rows = slice(sbi * sbr, (sbi + 1) * sbr)
        for h, sl in enumerate(sls):
            o_in[h].append(_dot_nt(q_sb[rows, sl].astype(BF16), st[h].astype(BF16)))
        kv = [_dot_tn(v[rows, sl].astype(BF16), k_end[rows, sl].astype(BF16)) for sl in sls]
        st = [st[h] * dec_sb[sbi][:, sl] + kv[h] for h, sl in enumerate(sls)]
    for h in range(heads):
        st_ref[h] = st[h]

    g = g_ref[0]
    gn = gn_ref[...]
    for h, sl in enumerate(sls):
        oh = o[h] + jnp.concatenate(o_in[h], axis=0)
        oh = oh * lax.rsqrt(jnp.mean(oh * oh, axis=-1, keepdims=True) + NORM_EPS)
        gh = g[:, sl]
        o_ref[0, :, sl] = (oh * gn[:, sl] * (gh * _sigmoid(gh))).astype(BF16)

    @pl.when(t == pl.num_programs(1) - 1)
    def _():
        for h in range(heads):
            sout_ref[0, h] = st_ref[h].T


def hgrn_scan(proj, hg_lb, g_norm, s0, *, layer, tc=128):
    b, t, _ = proj.shape
    d = D_MODEL
    tc = min(tc, t)
    has_state = s0 is not None
    col = lambda c: pl.BlockSpec((1, tc, d), lambda i, j, c=c: (i, j, c))
    in_specs = [col(0), col(1), col(2), col(3),
                pl.BlockSpec(hg_lb.shape, lambda i, j: (0, 0)),
                pl.BlockSpec((1, d), lambda i, j: (0, 0))]
    args = [proj, proj, proj, proj, hg_lb, g_norm.reshape(1, d)]
    st_spec = pl.BlockSpec((1, HG_HEADS, HG_DK, HG_DK), lambda i, j: (i, 0, 0, 0))
    if has_state:
        in_specs.append(st_spec)
        args.append(s0)
    return pl.pallas_call(
        functools.partial(_hgrn_body, tc=tc, layer=layer, has_state=has_state),
        grid=(b, t // tc),
        in_specs=in_specs,
        out_specs=[pl.BlockSpec((1, tc, d), lambda i, j: (i, j, 0)), st_spec],
        out_shape=[jax.ShapeDtypeStruct((b, t, d), BF16),
                   jax.ShapeDtypeStruct((b, HG_HEADS, HG_DK, HG_DK), F32)],
        scratch_shapes=[pltpu.VMEM((HG_HEADS, HG_DK, HG_DK), F32)],
        compiler_params=_cparams("parallel", "arbitrary"),
        name="hgrn_scan",
    )(*args)


def _rope_body(cos_ref, sin_ref, *, pos0, tc):
    half = RET_DK // 2
    pos = (pos0 + pl.program_id(0) * tc + _iota((tc, half), 0)).astype(F32)
    inv = jnp.power(jnp.float32(ROPE_BASE), -(_iota((tc, half), 1).astype(F32) / half))
    ang = pos * inv
    cos_ref[...] = jnp.cos(ang)
    sin_ref[...] = jnp.sin(ang)


def rope_table(pos0, t, *, tc=256):
    tc = min(tc, t)
    half = RET_DK // 2
    return pl.pallas_call(
        functools.partial(_rope_body, pos0=pos0, tc=tc),
        grid=(t // tc,),
        out_specs=[pl.BlockSpec((tc, half), lambda i: (i, 0))] * 2,
        out_shape=[jax.ShapeDtypeStruct((t, half), F32)] * 2,
        compiler_params=_cparams("parallel"),
        name="rope_table",
    )()


def _ret_body(*refs, tc, has_state):
    if has_state:
        q_ref, k_ref, v_ref, g_ref, cos_ref, sin_ref, gn_ref, s0_ref, o_ref, sout_ref, s_ref = refs
    else:
        q_ref, k_ref, v_ref, g_ref, cos_ref, sin_ref, gn_ref, o_ref, sout_ref, s_ref = refs
    t = pl.program_id(1)
    nh, dk, dv = RET_HEADS, RET_DK, RET_DV
    half = dk // 2

    @pl.when(t == 0)
    def _():
        s_ref[...] = s0_ref[0] if has_state else jnp.zeros((nh, dk, dv), F32)

    cos, sin = cos_ref[...], sin_ref[...]

    def rot(x):
        x1, x2 = x[:, :half], x[:, half:]
        return jnp.concatenate([x1 * cos - x2 * sin, x1 * sin + x2 * cos], axis=-1)

    heads = range(nh)
    lg = [math.log1p(-2.0 ** (-5.0 - h)) for h in heads]
    ksl = [slice(h * dk, (h + 1) * dk) for h in heads]
    vsl = [slice(h * dv, (h + 1) * dv) for h in heads]
    diff = (_iota((tc, tc), 0) - _iota((tc, tc), 1)).astype(F32)
    row = _iota((tc, 1), 0).astype(F32)
    q = [rot(q_ref[0, :, sl]) for sl in ksl]
    k = [rot(k_ref[0, :, sl]) * (dk ** -0.5) for sl in ksl]
    v = [v_ref[0, :, sl].astype(BF16) for sl in vsl]
    s = [s_ref[h] for h in heads]
    sc = [_dot_nt(q[h].astype(BF16), k[h].astype(BF16)) for h in heads]
    sc = [(jnp.where(diff >= 0.0, jnp.exp(diff * lg[h]), 0.0) * sc[h]).astype(BF16) for h in heads]
    o = [_dot(sc[h], v[h]) + _dot((q[h] * jnp.exp((row + 1.0) * lg[h])).astype(BF16), s[h].astype(BF16))
         for h in heads]
    kv = [_dot_tn((k[h] * jnp.exp((tc - 1.0 - row) * lg[h])).astype(BF16), v[h]) for h in heads]
    for h in heads:
        s_ref[h] = s[h] * math.exp(tc * lg[h]) + kv[h]
        oh = o[h] * lax.rsqrt(jnp.mean(o[h] * o[h], axis=-1, keepdims=True) + NORM_EPS)
        g = g_ref[0, :, vsl[h]]
        o_ref[0, :, vsl[h]] = (oh * gn_ref[:, vsl[h]] * (g * _sigmoid(g))).astype(BF16)

    @pl.when(t == pl.num_programs(1) - 1)
    def _():
        sout_ref[0] = s_ref[...]


def ret_scan(proj, cos, sin, g_norm, s0, *, tc=256):
    b, t, _ = proj.shape
    tc = min(tc, t)
    has_state = s0 is not None
    nh, dk, dv = RET_HEADS, RET_DK, RET_DV
    half = dk // 2
    in_specs = [pl.BlockSpec((1, tc, nh * dk), lambda i, j: (i, j, 0)),
                pl.BlockSpec((1, tc, nh * dk), lambda i, j: (i, j, 1)),
                pl.BlockSpec((1, tc, nh * dv), lambda i, j: (i, j, 1)),
                pl.BlockSpec((1, tc, nh * dv), lambda i, j: (i, j, 2)),
                pl.BlockSpec((tc, half), lambda i, j: (j, 0)),
                pl.BlockSpec((tc, half), lambda i, j: (j, 0)),
                _const_spec((1, nh * dv))]
    args = [proj, proj, proj, proj, cos, sin, g_norm.reshape(1, nh * dv)]
    st_spec = pl.BlockSpec((1, nh, dk, dv), lambda i, j: (i, 0, 0, 0))
    if has_state:
        in_specs.append(st_spec)
        args.append(s0)
    return pl.pallas_call(
        functools.partial(_ret_body, tc=tc, has_state=has_state),
        grid=(b, t // tc),
        in_specs=in_specs,
        out_specs=[pl.BlockSpec((1, tc, nh * dv), lambda i, j: (i, j, 0)), st_spec],
        out_shape=[jax.ShapeDtypeStruct((b, t, nh * dv), BF16),
                   jax.ShapeDtypeStruct((b, nh, dk, dv), F32)],
        scratch_shapes=[pltpu.VMEM((nh, dk, dv), F32)],
        compiler_params=_cparams("parallel", "arbitrary"),
        name="ret_scan",
    )(*args)


def _rwkv_proj_body(*refs, has_state):
    if has_state:
        (x_ref, nw_ref, sh0_ref, mu_ref, wrkv_ref, w0_ref, w1_ref, w2_ref, a0_ref, a1_ref, a2_ref,
         g1_ref, g2_ref, kk_ref, ka_ref, hsum_ref, hexp_ref,
         r_o, k_o, v_o, kk_o, a_o, lw_o, g_o, sh_o, carry_ref) = refs
    else:
        (x_ref, nw_ref, mu_ref, wrkv_ref, w0_ref, w1_ref, w2_ref, a0_ref, a1_ref, a2_ref,
         g1_ref, g2_ref, kk_ref, ka_ref, hsum_ref, hexp_ref,
         r_o, k_o, v_o, kk_o, a_o, lw_o, g_o, sh_o, carry_ref) = refs
    t = pl.program_id(1)
    tm = x_ref.shape[1]

    @pl.when(t == 0)
    def _():
        carry_ref[...] = sh0_ref[0] if has_state else jnp.zeros((1, D_MODEL), F32)

    h = _rms(x_ref[0], nw_ref[...])
    prev = jnp.where(_iota((tm, 1), 0) == 0, carry_ref[...], pltpu.roll(h, 1, axis=0))
    last = h[tm - 1:tm, :]
    carry_ref[...] = last
    sh_o[0] = last
    d = prev - h
    mix = lambda i: (h + d * mu_ref[i:i + 1, :]).astype(BF16)
    r = _dot(mix(0), wrkv_ref[0])
    k = _dot(mix(1), wrkv_ref[1])
    v = _dot(mix(2), wrkv_ref[2])
    w_pre = w0_ref[...] + _dot(jnp.tanh(_dot(mix(3), w1_ref[...])).astype(BF16), w2_ref[...])
    lw_o[0] = -math.exp(-0.5) * _sigmoid(w_pre)
    a = _sigmoid(a0_ref[...] + _dot(_dot(mix(4), a1_ref[...]).astype(BF16), a2_ref[...]))
    g_o[0] = _dot(_sigmoid(_dot(mix(5), g1_ref[...])).astype(BF16), g2_ref[...])
    kk = k * kk_ref[...]
    ss = _dot_sel(kk * kk, hsum_ref[...], pieces=2)
    inv = 1.0 / jnp.maximum(jnp.sqrt(ss), 1e-12)
    kk_o[0] = kk * _dot_sel(inv, hexp_ref[...], pieces=2)
    r_o[0] = r
    k_o[0] = k * (1.0 + (a - 1.0) * ka_ref[...])
    v_o[0] = v
    a_o[0] = a


def rwkv_proj(x, nw, shift0, p, *, tm=256):
    b, t, d = x.shape
    tm = min(tm, t)
    has_state = shift0 is not None
    row = lambda a: a.reshape(1, d)
    head_of_lane = jnp.arange(d) // RW_HEAD
    hsum = (head_of_lane[:, None] == jnp.arange(RW_HEADS)[None, :]).astype(BF16)
    hexp = hsum.T
    full = lambda a: pl.BlockSpec(a.shape, lambda i, j, n=a.ndim: (0,) * n)
    tok = pl.BlockSpec((1, tm, d), lambda i, j: (i, j, 0))
    args = [x, row(nw)]
    in_specs = [tok, full(row(nw))]
    if has_state:
        args.append(shift0.reshape(b, 1, d))
        in_specs.append(pl.BlockSpec((1, 1, d), lambda i, j: (i, 0, 0)))
    consts = [p['mu'], p['w_rkv'], row(p['w0']), p['w1'], p['w2'], row(p['a0']), p['a1'], p['a2'],
              p['g1'], p['g2'], row(p['k_k']), row(p['k_a']), hsum, hexp]
    args += consts
    in_specs += [full(a) for a in consts]
    outs = pl.pallas_call(
        functools.partial(_rwkv_proj_body, has_state=has_state),
        grid=(b, t // tm),
        in_specs=in_specs,
        out_specs=[tok] * 7 + [pl.BlockSpec((1, 1, d), lambda i, j: (i, 0, 0))],
        out_shape=[jax.ShapeDtypeStruct((b, t, d), F32)] * 7 + [jax.ShapeDtypeStruct((b, 1, d), F32)],
        scratch_shapes=[pltpu.VMEM((1, d), F32)],
        compiler_params=_cparams("parallel", "arbitrary"),
        name="rwkv_proj",
    )(*args)
    return outs[:7], outs[7].reshape(b, d)


def _rwkv_scan_body(*refs, L, has_state):
    if has_state:
        (r_ref, k_ref, v_ref, kk_ref, a_ref, lw_ref, g_ref, rk_ref, lnw_ref, lnb_ref, s0_ref,
         o_ref, sout_ref, s_ref) = refs
    else:
        (r_ref, k_ref, v_ref, kk_ref, a_ref, lw_ref, g_ref, rk_ref, lnw_ref, lnb_ref,
         o_ref, sout_ref, s_ref) = refs
    t = pl.program_id(1)
    gw = RW_GROUP * RW_HEAD
    ngroups = RW_HEADS // RW_GROUP
    sl_l, sl_h = _log2(L), _log2(RW_HEAD)
    n = RW_GROUP * L

    @pl.when(t == 0)
    def _():
        s_ref[...] = s0_ref[...] if has_state else jnp.zeros(s_ref.shape, F32)

    nrows = lw_ref.shape[0]
    tri = (_iota((L, L), 1) <= _iota((L, L), 0)).astype(BF16)
    v_r, a_hat_r, r_hat_r, b_hat_r, k_hat_r, b_til_r, k_til_r, gam_l_r, bonus_r = ([] for _ in range(9))
    for rb in range(nrows):
        lw = lw_ref[rb]
        cum = _sel_dot(tri, lw)
        cum_l = cum[L - 1:L, :]
        e_pos, e_neg = jnp.exp(cum), jnp.exp(-cum)
        e_tail = jnp.exp(cum_l - cum)
        kk, a, r, k = kk_ref[rb], a_ref[rb], r_ref[rb], k_ref[rb]
        beta = kk * a
        v_r.append(v_ref[rb])
        a_hat_r.append(-kk * jnp.exp(cum - lw))
        r_hat_r.append(r * e_pos)
        b_hat_r.append(beta * e_neg)
        k_hat_r.append(k * e_neg)
        b_til_r.append(beta * e_tail)
        k_til_r.append(k * e_tail)
        gam_l_r.append(jnp.exp(cum_l))
        bonus_r.append(r * k * rk_ref[...])

    own = (_iota((n, 1), 0) >> sl_l) == (_iota((1, gw), 1) >> sl_h)
    tile = lambda x: jnp.concatenate([x] * RW_GROUP, axis=0)
    stack = lambda x: jnp.where(own, tile(x), 0.0).astype(BF16)
    same = (_iota((n, n), 0) >> sl_l) == (_iota((n, n), 1) >> sl_l)
    to_bd = lambda x: jnp.where(same, tile(x), 0.0).astype(BF16)
    row_c, col_c = _iota((L, n), 0), _iota((L, n), 1) & (L - 1)
    strict_c, incl_c = col_c < row_c, col_c <= row_c
    eye_c = (col_c == row_c).astype(F32)
    bdg = (_iota((gw, gw), 0) >> sl_h) == (_iota((gw, gw), 1) >> sl_h)
    bdg16 = bdg.astype(BF16)
    inv_n = 1.0 / RW_HEAD

    units = [(rb, gi) for rb in range(nrows) for gi in range(ngroups)]
    groups = range(len(units))
    sls = [slice(gi * gw, (gi + 1) * gw) for _, gi in units]
    pick = lambda per_row: [per_row[rb][:, sl] for (rb, _), sl in zip(units, sls)]
    v_u, b_til_u, k_til_u, bonus_u = pick(v_r), pick(b_til_r), pick(k_til_r), pick(bonus_r)
    gam_l_u = pick(gam_l_r)
    ar = [jnp.concatenate([x, y], axis=0).astype(BF16) for x, y in zip(pick(a_hat_r), pick(r_hat_r))]
    c = [_dot_nt(ar[u], jnp.concatenate([stack(x), stack(y)], axis=0))
         for u, (x, y) in enumerate(zip(pick(b_hat_r), pick(k_hat_r)))]
    p_c = [jnp.where(strict_c, x[:L, :n], 0.0) for x in c]
    m_k = [jnp.concatenate([jnp.where(strict_c, x[:L, n:], 0.0), jnp.where(incl_c, x[L:, n:], 0.0)],
                           axis=0).astype(BF16) for x in c]
    m_rb = [jnp.where(incl_c, x[L:, :n], 0.0).astype(BF16) for x in c]
    t_c = [eye_c + x for x in p_c]
    for j in range(sl_l):
        p_bd = [to_bd(x) for x in p_c]
        if j == 0:
            p_c = [_dot(p_c[gi].astype(BF16), p_bd[gi]) for gi in groups]
        elif j < sl_l - 1:
            both = [_dot(jnp.concatenate([p_c[gi], t_c[gi]], axis=0).astype(BF16), p_bd[gi]) for gi in groups]
            p_c = [x[:L] for x in both]
            t_c = [t_c[gi] + both[gi][L:] for gi in groups]
        else:
            t_c = [t_c[gi] + _dot(t_c[gi].astype(BF16), p_bd[gi]) for gi in groups]
    s_g = [s_ref[rb, gi] for rb, gi in units]
    q = [_dot_nt(ar[u], s_g[u].astype(BF16)) for u in groups]
    kv = [_dot(m_k[u], stack(v_u[u])) for u in groups]
    u_c = [_dot(t_c[u].astype(BF16), stack(q[u][:L] + kv[u][:L])) for u in groups]
    y = [q[u][L:] + kv[u][L:] + _dot(m_rb[u], stack(u_c[u])) for u in groups]
    for u, (rb, gi) in enumerate(units):
        upd = _dot_tn(jnp.concatenate([u_c[u], v_u[u]], axis=0).astype(BF16),
                      jnp.concatenate([b_til_u[u], k_til_u[u]], axis=0).astype(BF16))
        s_ref[rb, gi] = s_g[u] * gam_l_u[u] + jnp.where(bdg, upd, 0.0)
    pieces = []
    for u in groups:
        pieces += _split3(y[u]) + _split3(y[u] * y[u]) + _split3(bonus_u[u])
    sums = _dot(jnp.concatenate(pieces, axis=0), bdg16)
    for u, (rb, gi) in enumerate(units):
        sl = sls[u]
        sum3 = lambda i, base=9 * u * L: sums[base + 3 * i * L:base + (3 * i + 1) * L] \
            + sums[base + (3 * i + 1) * L:base + (3 * i + 2) * L] + sums[base + (3 * i + 2) * L:base + (3 * i + 3) * L]
        mu = sum3(0) * inv_n
        var = sum3(1) * inv_n - mu * mu
        yn = (y[u] - mu) * lax.rsqrt(var + RW_GN_EPS) * lnw_ref[:, sl] + lnb_ref[:, sl]
        o_ref[rb, :, sl] = ((yn + sum3(2) * v_u[u]) * g_ref[rb, :, sl]).astype(BF16)

    @pl.when(t == pl.num_programs(1) - 1)
    def _():
        sout_ref[...] = s_ref[...]


def rwkv_scan(streams, r_k, ln_w, ln_b, s0_bd, *, chunk=64, rows=2):
    b, t, d = streams[0].shape
    L = min(chunk, t)
    nrows = rows if b % rows == 0 else 1
    has_state = s0_bd is not None
    ngroups = RW_HEADS // RW_GROUP
    gw = RW_GROUP * RW_HEAD
    tok = pl.BlockSpec((nrows, L, d), lambda i, j: (i, j, 0))
    vec = pl.BlockSpec((1, d), lambda i, j: (0, 0))
    st_spec = pl.BlockSpec((nrows, ngroups, gw, gw), lambda i, j: (i, 0, 0, 0))
    args = list(streams) + [r_k.reshape(1, d), ln_w.reshape(1, d), ln_b.reshape(1, d)]
    in_specs = [tok] * 7 + [vec] * 3
    if has_state:
        args.append(s0_bd)
        in_specs.append(st_spec)
    return pl.pallas_call(
        functools.partial(_rwkv_scan_body, L=L, has_state=has_state),
        grid=(b // nrows, t // L),
        in_specs=in_specs,
        out_specs=[tok, st_spec],
        out_shape=[jax.ShapeDtypeStruct((b, t, d), BF16),
                   jax.ShapeDtypeStruct((b, ngroups, gw, gw), F32)],
        scratch_shapes=[pltpu.VMEM((nrows, ngroups, gw, gw), F32)],
        compiler_params=_cparams("parallel", "arbitrary"),
        name="rwkv_scan",
    )(*args)


def _rwkv_state_to_bd(s):
    b = s.shape[0]
    g, m, hd = RW_HEADS // RW_GROUP, RW_GROUP, RW_HEAD
    s = s.reshape(b, g, m, hd, 1, hd) * jnp.eye(m, dtype=s.dtype)[None, None, :, None, :, None]
    return s.reshape(b, g, m * hd, m * hd)


def _rwkv_state_from_bd(sbd):
    b = sbd.shape[0]
    g, m, hd = RW_HEADS // RW_GROUP, RW_GROUP, RW_HEAD
    s = sbd.reshape(b, g, m, hd, m, hd)
    s = jnp.stack([s[:, :, i, :, i, :] for i in range(m)], axis=2)
    return s.reshape(b, RW_HEADS, hd, hd)


def _lru_body(*refs, tc, has_state):
    if has_state:
        (gate_ref, xb_ref, cw_ref, cb_ref, wg_ref, bg_ref, lam_ref, h0_ref, c0_ref,
         y_ref, hout_ref, xpad_ref, hc_ref) = refs
    else:
        (gate_ref, xb_ref, cw_ref, cb_ref, wg_ref, bg_ref, lam_ref,
         y_ref, hout_ref, xpad_ref, hc_ref) = refs
    t = pl.program_id(1)
    d = D_MODEL
    pad = 8

    @pl.when(t == 0)
    def _():
        xpad_ref[0:pad, :] = c0_ref[0] if has_state else jnp.zeros((pad, d), F32)
        hc_ref[...] = h0_ref[0] if has_state else jnp.zeros((1, d), F32)

    xpad_ref[pad:pad + tc, :] = xb_ref[0]
    xc = cb_ref[...]
    for j in range(CONV_W):
        off = pad - (CONV_W - 1) + j
        xc = xc + xpad_ref[off:off + tc, :] * cw_ref[j:j + 1, :]
    xpad_ref[0:pad, :] = xpad_ref[tc:tc + pad, :]

    xcb = xc.astype(BF16)
    r_pre, i_pre = [], []
    for nb in range(LRU_BLOCKS):
        sl = slice(nb * LRU_BW, (nb + 1) * LRU_BW)
        r_pre.append(_dot(xcb[:, sl], wg_ref[0, nb]))
        i_pre.append(_dot(xcb[:, sl], wg_ref[1, nb]))
    r_gate = _sigmoid(jnp.concatenate(r_pre, axis=-1) + bg_ref[0:1, :])
    i_gate = _sigmoid(jnp.concatenate(i_pre, axis=-1) + bg_ref[1:2, :])
    log_a = -LRU_C * r_gate * _softplus(-lam_ref[...])
    a = jnp.exp(log_a)
    om = -jnp.tanh(log_a) * (a * a + 1.0)
    bv = jnp.where(om > 0.0, om * lax.rsqrt(om), 0.0) * (i_gate * xc)

    grp = 8
    a = a.reshape(tc // grp, grp, d)
    bv = bv.reshape(tc // grp, grp, d)
    sub = _iota((1, grp, 1), 1)
    s = 1
    while s < grp:
        keep = sub >= s
        a_sh = jnp.where(keep, pltpu.roll(a, s, axis=1), 1.0)
        b_sh = jnp.where(keep, pltpu.roll(bv, s, axis=1), 0.0)
        bv = bv + a * b_sh
        a = a * a_sh
        s *= 2
    carry = hc_ref[...]
    rows = []
    for gi in range(tc // grp):
        h_g = bv[gi] + a[gi] * carry
        carry = h_g[grp - 1:grp, :]
        rows.append(h_g)
    hs = jnp.concatenate(rows, axis=0)
    last = carry
    hc_ref[...] = last
    hout_ref[0] = last
    y_ref[0] = (hs * jax.nn.gelu(gate_ref[0], approximate=True)).astype(BF16)


def lru_scan(proj, conv_w, conv_b, w_gates, b_gates, lam, h0, conv0, *, tc=256):
    b, t, _ = proj.shape
    d = D_MODEL
    tc = min(tc, t)
    has_state = h0 is not None
    full = lambda a: pl.BlockSpec(a.shape, lambda i, j, n=a.ndim: (0,) * n)
    consts = [conv_w, conv_b.reshape(1, d), w_gates, b_gates, lam.reshape(1, d)]
    args = [proj, proj] + consts
    in_specs = [pl.BlockSpec((1, tc, d), lambda i, j: (i, j, 0)),
                pl.BlockSpec((1, tc, d), lambda i, j: (i, j, 1))] + [full(a) for a in consts]
    if has_state:
        c0 = jnp.pad(conv0, ((0, 0), (8 - (CONV_W - 1), 0), (0, 0)))
        args += [h0.reshape(b, 1, d), c0]
        in_specs += [pl.BlockSpec((1, 1, d), lambda i, j: (i, 0, 0)),
                     pl.BlockSpec((1, 8, d), lambda i, j: (i, 0, 0))]
    y, hl = pl.pallas_call(
        functools.partial(_lru_body, tc=tc, has_state=has_state),
        grid=(b, t // tc),
        in_specs=in_specs,
        out_specs=[pl.BlockSpec((1, tc, d), lambda i, j: (i, j, 0)),
                   pl.BlockSpec((1, 1, d), lambda i, j: (i, 0, 0))],
        out_shape=[jax.ShapeDtypeStruct((b, t, d), BF16), jax.ShapeDtypeStruct((b, 1, d), F32)],
        scratch_shapes=[pltpu.VMEM((tc + 8, d), F32), pltpu.VMEM((1, d), F32)],
        compiler_params=_cparams("parallel", "arbitrary"),
        name="lru_scan",
    )(*args)
    return y, hl.reshape(b, d)


def _trunk(x, pos0, st, p):
    b, t, d = x.shape
    m = b * t
    get = (lambda name: st[name][0]) if st is not None else (lambda name: None)
    x2 = x.reshape(m, d)

    proj = norm_matmul(x2, p['norm_mix'][0], p['hg_w_in']).reshape(b, t, -1)
    o, hg_s = hgrn_scan(proj, p['hg_lb'], p['hg_norm'], get('hgrn'), layer=0)
    x2 = mix_out_ffn(o.reshape(m, d), p['hg_w_out'], x2, p['norm_ffn'][0], p['ffn_w_in'][0], p['ffn_w_out'][0],
                     p['norm_final'], final_norm=False)

    proj = norm_matmul(x2, p['norm_mix'][1], p['ret_w_in']).reshape(b, t, -1)
    cos, sin = rope_table(pos0, t)
    o, ret_s = ret_scan(proj, cos, sin, p['ret_norm'], get('ret'))
    x2 = mix_out_ffn(o.reshape(m, -1), p['ret_w_out'], x2, p['norm_ffn'][1], p['ffn_w_in'][1], p['ffn_w_out'][1],
                     p['norm_final'], final_norm=False)

    s0 = get('rwkv')
    streams, shift = rwkv_proj(x2.reshape(b, t, d), p['norm_mix'][2], get('shift'), p['rw'])
    o, rw_bd = rwkv_scan(streams, p['rw']['r_k'], p['rw']['ln_w'], p['rw']['ln_b'],
                         None if s0 is None else _rwkv_state_to_bd(s0))
    rw_s = _rwkv_state_from_bd(rw_bd)
    x2 = mix_out_ffn(o.reshape(m, d), p['rw']['w_out'], x2, p['norm_ffn'][2], p['ffn_w_in'][2], p['ffn_w_out'][2],
                     p['norm_final'], final_norm=False)

    proj = norm_matmul(x2, p['norm_mix'][3], p['lru_w_in']).reshape(b, t, -1)
    o, lru_s = lru_scan(proj, p['lru_conv_w'], p['lru_conv_b'], p['lru_w_gates'], p['lru_b_gates'],
                        p['lru_lambda'], get('lru'), get('conv'))
    if st is None:
        conv_s = proj[:, t - (CONV_W - 1):, d:]
    else:
        conv_s = jnp.concatenate([st['conv'][0], proj[:, :, d:]], axis=1)[:, -(CONV_W - 1):]
    y = mix_out_ffn(o.reshape(m, d), p['lru_w_out'], x2, p['norm_ffn'][3], p['ffn_w_in'][3], p['ffn_w_out'][3],
                    p['norm_final'], final_norm=True)
    return (y.reshape(b, t, d), hg_s[None], ret_s[None], rw_s[None], shift[None], lru_s[None], conv_s[None])


def kernel(x_prompt, x_sample, state_hgrn, state_ret, state_rwkv, state_rwkv_shift, state_lru, state_lru_conv, norm_mix, norm_ffn, norm_final, hg_lb, hg_w_in, hg_norm, hg_w_out, ret_w_in, ret_norm, ret_w_out, rw_mu, rw_w_rkv, rw_w0, rw_w1, rw_w2, rw_a0, rw_a1, rw_a2, rw_g1, rw_g2, rw_k_k, rw_k_a, rw_r_k, rw_ln_w, rw_ln_b, rw_w_out, lru_w_in, lru_conv_w, lru_conv_b, lru_w_gates, lru_b_gates, lru_lambda, lru_w_out, ffn_w_in, ffn_w_out):
    bf = lambda a: a.astype(BF16)
    p = dict(
        norm_mix=norm_mix, norm_ffn=norm_ffn, norm_final=norm_final,
        hg_lb=hg_lb, hg_w_in=bf(hg_w_in[0]), hg_norm=hg_norm[0], hg_w_out=bf(hg_w_out[0]),
        ret_w_in=bf(ret_w_in[0]), ret_norm=ret_norm[0], ret_w_out=bf(ret_w_out[0]),
        rw=dict(mu=rw_mu[0], w_rkv=bf(rw_w_rkv[0]), w0=rw_w0[0], w1=bf(rw_w1[0]), w2=bf(rw_w2[0]),
                a0=rw_a0[0], a1=bf(rw_a1[0]), a2=bf(rw_a2[0]), g1=bf(rw_g1[0]), g2=bf(rw_g2[0]),
                k_k=rw_k_k[0], k_a=rw_k_a[0], r_k=rw_r_k[0], ln_w=rw_ln_w[0], ln_b=rw_ln_b[0],
                w_out=bf(rw_w_out[0])),
        lru_w_in=bf(lru_w_in[0]), lru_conv_w=lru_conv_w[0], lru_conv_b=lru_conv_b[0],
        lru_w_gates=bf(lru_w_gates[0]), lru_b_gates=lru_b_gates[0], lru_lambda=lru_lambda[0],
        lru_w_out=bf(lru_w_out[0]),
        ffn_w_in=bf(ffn_w_in), ffn_w_out=bf(ffn_w_out),
    )
    yp, hg_p, ret_p, rw_p, sh_p, lru_p, conv_p = _trunk(x_prompt, 0, None, p)
    st = dict(hgrn=state_hgrn, ret=state_ret, rwkv=state_rwkv, shift=state_rwkv_shift,
              lru=state_lru, conv=state_lru_conv)
    ys, hg_s, ret_s, rw_s, sh_s, lru_s, conv_s = _trunk(x_sample, PAST_LEN, st, p)
    return (yp, ys, hg_p, hg_s, ret_p, ret_s, rw_p, rw_s, sh_p, sh_s, lru_p, lru_s, conv_p, conv_s)
```

```python
import functools
import math

import jax
import jax.numpy as jnp
from jax import lax
from jax.experimental import pallas as pl
from jax.experimental.pallas import tpu as pltpu

F32 = jnp.float32
BF16 = jnp.bfloat16

D_MODEL = 1024
NORM_EPS = 1e-6
HG_HEADS, HG_DK, HG_BLOCK = 8, 128, 16
HG_SUPER = 4
RET_HEADS, RET_DK, RET_DV = 4, 256, 512
ROPE_BASE = 10000.0
RW_HEADS, RW_HEAD = 16, 64
RW_GROUP = 4
RW_GN_EPS = 64e-5
LRU_BLOCKS, LRU_BW, CONV_W, LRU_C = 8, 128, 4, 8.0
D_FF = 2816
PAST_LEN = 2048

VMEM_LIMIT_BYTES = 56 * 1024 * 1024


def _cparams(*sem):
    return pltpu.CompilerParams(dimension_semantics=sem, vmem_limit_bytes=VMEM_LIMIT_BYTES)


def _const_spec(shape):
    n = len(shape)
    return pl.BlockSpec(tuple(shape), lambda *_: (0,) * n, pipeline_mode=pl.Buffered(1))


def _iota(shape, dim):
    return lax.broadcasted_iota(jnp.int32, shape, dim)


def _dot(a, b):
    return jnp.dot(a, b, preferred_element_type=F32)


def _dot_nt(a, b):
    return lax.dot_general(a, b, (((1,), (1,)), ((), ())), preferred_element_type=F32)


def _dot_tn(a, b):
    return lax.dot_general(a, b, (((0,), (0,)), ((), ())), preferred_element_type=F32)


def _split3(x):
    hi = x.astype(BF16)
    r1 = x - hi.astype(F32)
    mid = r1.astype(BF16)
    lo = (r1 - mid.astype(F32)).astype(BF16)
    return hi, mid, lo


def _sel_dot(m01, x, pieces=3):
    return sum(_dot(m01, p) for p in _split3(x)[:pieces])


def _dot_sel(x, m01, pieces=3):
    return sum(_dot(p, m01) for p in _split3(x)[:pieces])


def _rms(x, w):
    return x * lax.rsqrt(jnp.mean(x * x, axis=-1, keepdims=True) + NORM_EPS) * w


def _sigmoid(x):
    return 0.5 * jnp.tanh(0.5 * x) + 0.5


def _softplus(x):
    return jnp.maximum(x, 0.0) + jnp.log1p(jnp.exp(-jnp.abs(x)))


def _log2(n):
    l = int(math.log2(n))
    assert 1 << l == n, n
    return l


def _norm_mm_body(x_ref, nw_ref, w_ref, o_ref, *, tn):
    h = _rms(x_ref[...], nw_ref[...]).astype(BF16)
    for c in range(w_ref.shape[1] // tn):
        o_ref[:, c * tn:(c + 1) * tn] = _dot(h, w_ref[:, c * tn:(c + 1) * tn]).astype(o_ref.dtype)


def norm_matmul(x2d, nw, w, *, out_dtype=F32, tm=512, tn=1024):
    m, k = x2d.shape
    n = w.shape[1]
    tm, tn = min(tm, m), min(tn, n)
    return pl.pallas_call(
        functools.partial(_norm_mm_body, tn=tn),
        grid=(m // tm,),
        in_specs=[pl.BlockSpec((tm, k), lambda i: (i, 0)),
                  _const_spec((1, k)),
                  _const_spec((k, n))],
        out_specs=pl.BlockSpec((tm, n), lambda i: (i, 0)),
        out_shape=jax.ShapeDtypeStruct((m, n), out_dtype),
        compiler_params=_cparams("parallel"),
        name="norm_matmul",
    )(x2d, nw.reshape(1, k), w)


def _mix_ffn_body(o_ref, wo_ref, x_ref, nw_ref, win_ref, wout_ref, fnw_ref, y_ref, *, final_norm):
    x1 = x_ref[...] + _dot(o_ref[...].astype(BF16), wo_ref[...])
    h = _rms(x1, nw_ref[...]).astype(BF16)
    g = _dot(h, win_ref[:, 0:D_FF])
    u = _dot(h, win_ref[:, D_FF:2 * D_FF])
    act = (g * _sigmoid(g) * u).astype(BF16)
    acc = x1 + _dot(act, wout_ref[...])
    if final_norm:
        acc = _rms(acc, fnw_ref[...])
    y_ref[...] = acc


def mix_out_ffn(o2d, w_o, x2d, nw, w_in, w_out, fnw, *, final_norm, tm=512):
    m, ko = o2d.shape
    d = x2d.shape[1]
    tm = min(tm, m)
    return pl.pallas_call(
        functools.partial(_mix_ffn_body, final_norm=final_norm),
        grid=(m // tm,),
        in_specs=[pl.BlockSpec((tm, ko), lambda i: (i, 0)),
                  _const_spec((ko, d)),
                  pl.BlockSpec((tm, d), lambda i: (i, 0)),
                  _const_spec((1, d)),
                  _const_spec((d, 2 * D_FF)),
                  _const_spec((D_FF, d)),
                  _const_spec((1, d))],
        out_specs=pl.BlockSpec((tm, d), lambda i: (i, 0)),
        out_shape=jax.ShapeDtypeStruct((m, d), F32),
        compiler_params=_cparams("parallel"),
        name="mix_out_ffn",
    )(o2d, w_o, x2d, nw.reshape(1, d), w_in, w_out, fnw.reshape(1, d))


def _hgrn_body(*refs, tc, layer, has_state):
    if has_state:
        q_ref, f_ref, v_ref, g_ref, lbp_ref, gn_ref, s0_ref, o_ref, sout_ref, st_ref = refs
    else:
        q_ref, f_ref, v_ref, g_ref, lbp_ref, gn_ref, o_ref, sout_ref, st_ref = refs
    t = pl.program_id(1)
    blk = min(HG_BLOCK, tc)
    heads, dk = HG_HEADS, HG_DK
    nb = tc // blk
    m = min(HG_SUPER, nb)
    sbr = m * blk

    @pl.when(t == 0)
    def _():
        for h in range(heads):
            st_ref[h] = s0_ref[0, h].T if has_state else jnp.zeros((dk, dk), F32)

    lbp = lbp_ref[...]
    e = jnp.exp(lbp - jnp.max(lbp, axis=0, keepdims=True))
    lb = jnp.sum(e[:layer + 1], axis=0, keepdims=True) / jnp.sum(e, axis=0, keepdims=True)

    half_th = 0.5 * jnp.tanh(0.5 * f_ref[0])
    f = lb + (1.0 - lb) * (0.5 + half_th)
    k = (1.0 - lb) * (0.5 - half_th)
    lf = jnp.log(f)
    sh = _log2(blk)
    ri, ci = _iota((tc, tc), 0), _iota((tc, tc), 1)
    bdiff = (ri >> sh) - (ci >> sh)
    same_sb = (ri >> _log2(sbr)) == (ci >> _log2(sbr))
    causal = (bdiff == 0) & (ci <= ri)
    cum = _sel_dot(causal.astype(BF16), lf, pieces=2)
    tot = _sel_dot((bdiff == 0).astype(BF16), lf, pieces=2)
    q_in = q_ref[0] * jnp.exp(cum)
    k_in = k * jnp.exp(-cum)
    kst1 = k * jnp.exp(tot - cum)
    rows_of = lambda b: slice(b * blk, (b + 1) * blk)
    drow = [jnp.exp(tot[b * blk:b * blk + 1, :]) for b in range(nb)]
    kst = {1: [kst1[rows_of(b)] for b in range(nb)]}
    for dd in range(2, m + 1):
        kst[dd] = [kst[dd - 1][b] * drow[b + dd - 1] if (b % m) + dd - 1 <= m - 1 else None for b in range(nb)]
    zero_blk = jnp.zeros((blk, heads * dk), F32)
    k_far = [jnp.concatenate([x if x is not None else zero_blk for x in kst[dd]], axis=0) for dd in range(2, m)]
    k_end = jnp.concatenate([kst[m - (b % m)][b] for b in range(nb)], axis=0)
    q_sb = []
    for b in range(nb):
        x = q_in[rows_of(b)]
        for l in range(1, (b % m) + 1):
            x = x * drow[b - l]
        q_sb.append(x)
    q_sb = jnp.concatenate(q_sb, axis=0)
    dec_sb = []
    for sbi in range(nb // m):
        x = drow[sbi * m]
        for b in range(sbi * m + 1, (sbi + 1) * m):
            x = x * drow[b]
        dec_sb.append(x)

    v = v_ref[0]
    sls = [slice(h * dk, (h + 1) * dk) for h in range(heads)]
    rhs = [jnp.concatenate([k_in[:, sl], kst1[:, sl]] + [x[:, sl] for x in k_far], axis=0).astype(BF16) for sl in sls]
    sc = [_dot_nt(q_in[:, sl].astype(BF16), rhs[h]) for h, sl in enumerate(sls)]
    masks = [causal] + [same_sb & (bdiff == dd) for dd in range(1, m)]
    sc = [sum(jnp.where(masks[dd], x[:, dd * tc:(dd + 1) * tc], 0.0) for dd in range(m)).astype(BF16) for x in sc]
    o = [_dot(sc[h], v[:, sl].astype(BF16)) for h, sl in enumerate(sls)]
    st = [st_ref[h] for h in range(heads)]
    o_in = [[] for _ in range(heads)]
    for sbi in range(nb // m):
        rows = slice(sbi * sbr, (sbi + 1) * sbr)
        for h, sl in enumerate(sls):
            o_in[h].append(_dot_nt(q_sb[rows, sl].astype(BF16), st[h].astype(BF16)))
        kv = [_dot_tn(v[rows, sl].astype(BF16), k_end[rows, sl].astype(BF16)) for sl in sls]
        st = [st[h] * dec_sb[sbi][:, sl] + kv[h] for h, sl in enumerate(sls)]
    for h in range(heads):
        st_ref[h] = st[h]

    g = g_ref[0]
    gn = gn_ref[...]
    for h, sl in enumerate(sls):
        oh = o[h] + jnp.concatenate(o_in[h], axis=0)
        oh = oh * lax.rsqrt(jnp.mean(oh * oh, axis=-1, keepdims=True) + NORM_EPS)
        gh = g[:, sl]
        o_ref[0, :, sl] = (oh * gn[:, sl] * (gh * _sigmoid(gh))).astype(BF16)

    @pl.when(t == pl.num_programs(1) - 1)
    def _():
        for h in range(heads):
            sout_ref[0, h] = st_ref[h].T


def hgrn_scan(proj, hg_lb, g_norm, s0, *, layer, tc=128):
    b, t, _ = proj.shape
    d = D_MODEL
    tc = min(tc, t)
    has_state = s0 is not None
    col = lambda c: pl.BlockSpec((1, tc, d), lambda i, j, c=c: (i, j, c))
    in_specs = [col(0), col(1), col(2), col(3),
                pl.BlockSpec(hg_lb.shape, lambda i, j: (0, 0)),
                pl.BlockSpec((1, d), lambda i, j: (0, 0))]
    args = [proj, proj, proj, proj, hg_lb, g_norm.reshape(1, d)]
    st_spec = pl.BlockSpec((1, HG_HEADS, HG_DK, HG_DK), lambda i, j: (i, 0, 0, 0))
    if has_state:
        in_specs.append(st_spec)
        args.append(s0)
    return pl.pallas_call(
        functools.partial(_hgrn_body, tc=tc, layer=layer, has_state=has_state),
        grid=(b, t // tc),
        in_specs=in_specs,
        out_specs=[pl.BlockSpec((1, tc, d), lambda i, j: (i, j, 0)), st_spec],
        out_shape=[jax.ShapeDtypeStruct((b, t, d), BF16),
                   jax.ShapeDtypeStruct((b, HG_HEADS, HG_DK, HG_DK), F32)],
        scratch_shapes=[pltpu.VMEM((HG_HEADS, HG_DK, HG_DK), F32)],
        compiler_params=_cparams("parallel", "arbitrary"),
        name="hgrn_scan",
    )(*args)


def _rope_body(cos_ref, sin_ref, *, pos0, tc):
    half = RET_DK // 2
    pos = (pos0 + pl.program_id(0) * tc + _iota((tc, half), 0)).astype(F32)
    inv = jnp.power(jnp.float32(ROPE_BASE), -(_iota((tc, half), 1).astype(F32) / half))
    ang = pos * inv
    cos_ref[...] = jnp.cos(ang)
    sin_ref[...] = jnp.sin(ang)


def rope_table(pos0, t, *, tc=256):
    tc = min(tc, t)
    half = RET_DK // 2
    return pl.pallas_call(
        functools.partial(_rope_body, pos0=pos0, tc=tc),
        grid=(t // tc,),
        out_specs=[pl.BlockSpec((tc, half), lambda i: (i, 0))] * 2,
        out_shape=[jax.ShapeDtypeStruct((t, half), F32)] * 2,
        compiler_params=_cparams("parallel"),
        name="rope_table",
    )()


def _ret_body(*refs, tc, has_state):
    if has_state:
        q_ref, k_ref, v_ref, g_ref, cos_ref, sin_ref, gn_ref, s0_ref, o_ref, sout_ref, s_ref = refs
    else:
        q_ref, k_ref, v_ref, g_ref, cos_ref, sin_ref, gn_ref, o_ref, sout_ref, s_ref = refs
    t = pl.program_id(1)
    nh, dk, dv = RET_HEADS, RET_DK, RET_DV
    half = dk // 2

    @pl.when(t == 0)
    def _():
        s_ref[...] = s0_ref[0] if has_state else jnp.zeros((nh, dk, dv), F32)

    cos, sin = cos_ref[...], sin_ref[...]

    def rot(x):
        x1, x2 = x[:, :half], x[:, half:]
        return jnp.concatenate([x1 * cos - x2 * sin, x1 * sin + x2 * cos], axis=-1)

    heads = range(nh)
    lg = [math.log1p(-2.0 ** (-5.0 - h)) for h in heads]
    ksl = [slice(h * dk, (h + 1) * dk) for h in heads]
    vsl = [slice(h * dv, (h + 1) * dv) for h in heads]
    diff = (_iota((tc, tc), 0) - _iota((tc, tc), 1)).astype(F32)
    row = _iota((tc, 1), 0).astype(F32)
    q = [rot(q_ref[0, :, sl].astype(F32)) for sl in ksl]
    k = [rot(k_ref[0, :, sl].astype(F32)) * (dk ** -0.5) for sl in ksl]
    v = [v_ref[0, :, sl].astype(BF16) for sl in vsl]
    s = [s_ref[h] for h in heads]
    sc = [_dot_nt(q[h].astype(BF16), k[h].astype(BF16)) for h in heads]
    sc = [(jnp.where(diff >= 0.0, jnp.exp(diff * lg[h]), 0.0) * sc[h]).astype(BF16) for h in heads]
    o = [_dot(sc[h], v[h]) + _dot((q[h] * jnp.exp((row + 1.0) * lg[h])).astype(BF16), s[h].astype(BF16))
         for h in heads]
    kv = [_dot_tn((k[h] * jnp.exp((tc - 1.0 - row) * lg[h])).astype(BF16), v[h]) for h in heads]
    for h in heads:
        s_ref[h] = s[h] * math.exp(tc * lg[h]) + kv[h]
        oh = o[h] * lax.rsqrt(jnp.mean(o[h] * o[h], axis=-1, keepdims=True) + NORM_EPS)
        g = g_ref[0, :, vsl[h]].astype(F32)
        o_ref[0, :, vsl[h]] = (oh * gn_ref[:, vsl[h]] * (g * _sigmoid(g))).astype(BF16)

    @pl.when(t == pl.num_programs(1) - 1)
    def _():
        sout_ref[0] = s_ref[...]


def ret_scan(proj, cos, sin, g_norm, s0, *, tc=256):
    b, t, _ = proj.shape
    tc = min(tc, t)
    has_state = s0 is not None
    nh, dk, dv = RET_HEADS, RET_DK, RET_DV
    half = dk // 2
    in_specs = [pl.BlockSpec((1, tc, nh * dk), lambda i, j: (i, j, 0)),
                pl.BlockSpec((1, tc, nh * dk), lambda i, j: (i, j, 1)),
                pl.BlockSpec((1, tc, nh * dv), lambda i, j: (i, j, 1)),
                pl.BlockSpec((1, tc, nh * dv), lambda i, j: (i, j, 2)),
                pl.BlockSpec((tc, half), lambda i, j: (j, 0)),
                pl.BlockSpec((tc, half), lambda i, j: (j, 0)),
                _const_spec((1, nh * dv))]
    args = [proj, proj, proj, proj, cos, sin, g_norm.reshape(1, nh * dv)]
    st_spec = pl.BlockSpec((1, nh, dk, dv), lambda i, j: (i, 0, 0, 0))
    if has_state:
        in_specs.append(st_spec)
        args.append(s0)
    return pl.pallas_call(
        functools.partial(_ret_body, tc=tc, has_state=has_state),
        grid=(b, t // tc),
        in_specs=in_specs,
        out_specs=[pl.BlockSpec((1, tc, nh * dv), lambda i, j: (i, j, 0)), st_spec],
        out_shape=[jax.ShapeDtypeStruct((b, t, nh * dv), BF16),
                   jax.ShapeDtypeStruct((b, nh, dk, dv), F32)],
        scratch_shapes=[pltpu.VMEM((nh, dk, dv), F32)],
        compiler_params=_cparams("parallel", "arbitrary"),
        name="ret_scan",
    )(*args)


def _rwkv_proj_body(*refs, has_state):
    if has_state:
        (x_ref, nw_ref, sh0_ref, mu_ref, wrkv_ref, w0_ref, w1_ref, w2_ref, a0_ref, a1_ref, a2_ref,
         g1_ref, g2_ref, kk_ref, ka_ref, hsum_ref, hexp_ref,
         r_o, k_o, v_o, kk_o, a_o, lw_o, g_o, sh_o, carry_ref) = refs
    else:
        (x_ref, nw_ref, mu_ref, wrkv_ref, w0_ref, w1_ref, w2_ref, a0_ref, a1_ref, a2_ref,
         g1_ref, g2_ref, kk_ref, ka_ref, hsum_ref, hexp_ref,
         r_o, k_o, v_o, kk_o, a_o, lw_o, g_o, sh_o, carry_ref) = refs
    t = pl.program_id(1)
    tm = x_ref.shape[1]

    @pl.when(t == 0)
    def _():
        carry_ref[...] = sh0_ref[0] if has_state else jnp.zeros((1, D_MODEL), F32)

    h = _rms(x_ref[0], nw_ref[...])
    prev = jnp.where(_iota((tm, 1), 0) == 0, carry_ref[...], pltpu.roll(h, 1, axis=0))
    last = h[tm - 1:tm, :]
    carry_ref[...] = last
    sh_o[0] = last
    d = prev - h
    mix = lambda i: (h + d * mu_ref[i:i + 1, :]).astype(BF16)
    r = _dot(mix(0), wrkv_ref[0])
    k = _dot(mix(1), wrkv_ref[1])
    v = _dot(mix(2), wrkv_ref[2])
    w_pre = w0_ref[...] + _dot(jnp.tanh(_dot(mix(3), w1_ref[...])).astype(BF16), w2_ref[...])
    lw_o[0] = -math.exp(-0.5) * _sigmoid(w_pre)
    a = _sigmoid(a0_ref[...] + _dot(_dot(mix(4), a1_ref[...]).astype(BF16), a2_ref[...]))
    g_o[0] = _dot(_sigmoid(_dot(mix(5), g1_ref[...])).astype(BF16), g2_ref[...]).astype(g_o.dtype)
    kk = k * kk_ref[...]
    ss = _dot_sel(kk * kk, hsum_ref[...], pieces=2)
    inv = 1.0 / jnp.maximum(jnp.sqrt(ss), 1e-12)
    kk_o[0] = (kk * _dot_sel(inv, hexp_ref[...], pieces=2)).astype(kk_o.dtype)
    r_o[0] = r.astype(r_o.dtype)
    k_o[0] = (k * (1.0 + (a - 1.0) * ka_ref[...])).astype(k_o.dtype)
    v_o[0] = v.astype(v_o.dtype)
    a_o[0] = a


def rwkv_proj(x, nw, shift0, p, *, tm=256):
    b, t, d = x.shape
    tm = min(tm, t)
    has_state = shift0 is not None
    row = lambda a: a.reshape(1, d)
    head_of_lane = jnp.arange(d) // RW_HEAD
    hsum = (head_of_lane[:, None] == jnp.arange(RW_HEADS)[None, :]).astype(BF16)
    hexp = hsum.T
    full = lambda a: pl.BlockSpec(a.shape, lambda i, j, n=a.ndim: (0,) * n)
    tok = pl.BlockSpec((1, tm, d), lambda i, j: (i, j, 0))
    args = [x, row(nw)]
    in_specs = [tok, full(row(nw))]
    if has_state:
        args.append(shift0.reshape(b, 1, d))
        in_specs.append(pl.BlockSpec((1, 1, d), lambda i, j: (i, 0, 0)))
    consts = [p['mu'], p['w_rkv'], row(p['w0']), p['w1'], p['w2'], row(p['a0']), p['a1'], p['a2'],
              p['g1'], p['g2'], row(p['k_k']), row(p['k_a']), hsum, hexp]
    args += consts
    in_specs += [full(a) for a in consts]
    outs = pl.pallas_call(
        functools.partial(_rwkv_proj_body, has_state=has_state),
        grid=(b, t // tm),
        in_specs=in_specs,
        out_specs=[tok] * 7 + [pl.BlockSpec((1, 1, d), lambda i, j: (i, 0, 0))],
        out_shape=[jax.ShapeDtypeStruct((b, t, d), dt) for dt in (BF16, BF16, BF16, BF16, F32, F32, BF16)]
        + [jax.ShapeDtypeStruct((b, 1, d), F32)],
        scratch_shapes=[pltpu.VMEM((1, d), F32)],
        compiler_params=_cparams("parallel", "arbitrary"),
        name="rwkv_proj",
    )(*args)
    return outs[:7], outs[7].reshape(b, d)


def _rwkv_scan_body(*refs, L, has_state):
    if has_state:
        (r_ref, k_ref, v_ref, kk_ref, a_ref, lw_ref, g_ref, rk_ref, lnw_ref, lnb_ref, s0_ref,
         o_ref, sout_ref, s_ref) = refs
    else:
        (r_ref, k_ref, v_ref, kk_ref, a_ref, lw_ref, g_ref, rk_ref, lnw_ref, lnb_ref,
         o_ref, sout_ref, s_ref) = refs
    t = pl.program_id(1)
    gw = RW_GROUP * RW_HEAD
    ngroups = RW_HEADS // RW_GROUP
    sl_l, sl_h = _log2(L), _log2(RW_HEAD)
    n = RW_GROUP * L

    @pl.when(t == 0)
    def _():
        s_ref[...] = s0_ref[...] if has_state else jnp.zeros(s_ref.shape, F32)

    nrows = lw_ref.shape[0]
    tri = (_iota((L, L), 1) <= _iota((L, L), 0)).astype(BF16)
    v_r, a_hat_r, r_hat_r, b_hat_r, k_hat_r, b_til_r, k_til_r, gam_l_r, bonus_r = ([] for _ in range(9))
    for rb in range(nrows):
        lw = lw_ref[rb]
        cum = _sel_dot(tri, lw)
        cum_l = cum[L - 1:L, :]
        e_pos, e_neg = jnp.exp(cum), jnp.exp(-cum)
        e_tail = jnp.exp(cum_l - cum)
        kk, a, r, k = kk_ref[rb].astype(F32), a_ref[rb], r_ref[rb].astype(F32), k_ref[rb].astype(F32)
        beta = kk * a
        v_r.append(v_ref[rb].astype(F32))
        a_hat_r.append(-kk * jnp.exp(cum - lw))
        r_hat_r.append(r * e_pos)
        b_hat_r.append(beta * e_neg)
        k_hat_r.append(k * e_neg)
        b_til_r.append(beta * e_tail)
        k_til_r.append(k * e_tail)
        gam_l_r.append(jnp.exp(cum_l))
        bonus_r.append(r * k * rk_ref[...])

    own = (_iota((n, 1), 0) >> sl_l) == (_iota((1, gw), 1) >> sl_h)
    tile = lambda x: jnp.concatenate([x] * RW_GROUP, axis=0)
    stack = lambda x: jnp.where(own, tile(x), 0.0).astype(BF16)
    same = (_iota((n, n), 0) >> sl_l) == (_iota((n, n), 1) >> sl_l)
    to_bd = lambda x: jnp.where(same, tile(x), 0.0).astype(BF16)
    row_c, col_c = _iota((L, n), 0), _iota((L, n), 1) & (L - 1)
    strict_c, incl_c = col_c < row_c, col_c <= row_c
    eye_c = (col_c == row_c).astype(F32)
    bdg = (_iota((gw, gw), 0) >> sl_h) == (_iota((gw, gw), 1) >> sl_h)
    bdg16 = bdg.astype(BF16)
    inv_n = 1.0 / RW_HEAD

    units = [(rb, gi) for rb in range(nrows) for gi in range(ngroups)]
    groups = range(len(units))
    sls = [slice(gi * gw, (gi + 1) * gw) for _, gi in units]
    pick = lambda per_row: [per_row[rb][:, sl] for (rb, _), sl in zip(units, sls)]
    v_u, b_til_u, k_til_u, bonus_u = pick(v_r), pick(b_til_r), pick(k_til_r), pick(bonus_r)
    gam_l_u = pick(gam_l_r)
    ar = [jnp.concatenate([x, y], axis=0).astype(BF16) for x, y in zip(pick(a_hat_r), pick(r_hat_r))]
    c = [_dot_nt(ar[u], jnp.concatenate([stack(x), stack(y)], axis=0))
         for u, (x, y) in enumerate(zip(pick(b_hat_r), pick(k_hat_r)))]
    p_c = [jnp.where(strict_c, x[:L, :n], 0.0) for x in c]
    m_k = [jnp.concatenate([jnp.where(strict_c, x[:L, n:], 0.0), jnp.where(incl_c, x[L:, n:], 0.0)],
                           axis=0).astype(BF16) for x in c]
    m_rb = [jnp.where(incl_c, x[L:, :n], 0.0).astype(BF16) for x in c]
    t_c = [eye_c + x for x in p_c]
    for j in range(sl_l):
        p_bd = [to_bd(x) for x in p_c]
        if j == 0:
            p_c = [_dot(p_c[gi].astype(BF16), p_bd[gi]) for gi in groups]
        elif j < sl_l - 1:
            both = [_dot(jnp.concatenate([p_c[gi], t_c[gi]], axis=0).astype(BF16), p_bd[gi]) for gi in groups]
            p_c = [x[:L] for x in both]
            t_c = [t_c[gi] + both[gi][L:] for gi in groups]
        else:
            t_c = [t_c[gi] + _dot(t_c[gi].astype(BF16), p_bd[gi]) for gi in groups]
    s_g = [s_ref[rb, gi] for rb, gi in units]
    q = [_dot_nt(ar[u], s_g[u].astype(BF16)) for u in groups]
    kv = [_dot(m_k[u], stack(v_u[u])) for u in groups]
    u_c = [_dot(t_c[u].astype(BF16), stack(q[u][:L] + kv[u][:L])) for u in groups]
    y = [q[u][L:] + kv[u][L:] + _dot(m_rb[u], stack(u_c[u])) for u in groups]
    for u, (rb, gi) in enumerate(units):
        upd = _dot_tn(jnp.concatenate([u_c[u], v_u[u]], axis=0).astype(BF16),
                      jnp.concatenate([b_til_u[u], k_til_u[u]], axis=0).astype(BF16))
        s_ref[rb, gi] = s_g[u] * gam_l_u[u] + jnp.where(bdg, upd, 0.0)
    pieces = []
    for u in groups:
        pieces += _split3(y[u]) + _split3(y[u] * y[u]) + _split3(bonus_u[u])
    sums = _dot(jnp.concatenate(pieces, axis=0), bdg16)
    for u, (rb, gi) in enumerate(units):
        sl = sls[u]
        sum3 = lambda i, base=9 * u * L: sums[base + 3 * i * L:base + (3 * i + 1) * L] \
            + sums[base + (3 * i + 1) * L:base + (3 * i + 2) * L] + sums[base + (3 * i + 2) * L:base + (3 * i + 3) * L]
        mu = sum3(0) * inv_n
        var = sum3(1) * inv_n - mu * mu
        yn = (y[u] - mu) * lax.rsqrt(var + RW_GN_EPS) * lnw_ref[:, sl] + lnb_ref[:, sl]
        o_ref[rb, :, sl] = ((yn + sum3(2) * v_u[u]) * g_ref[rb, :, sl].astype(F32)).astype(BF16)

    @pl.when(t == pl.num_programs(1) - 1)
    def _():
        sout_ref[...] = s_ref[...]


def rwkv_scan(streams, r_k, ln_w, ln_b, s0_bd, *, chunk=64, rows=2):
    b, t, d = streams[0].shape
    L = min(chunk, t)
    nrows = rows if b % rows == 0 else 1
    has_state = s0_bd is not None
    ngroups = RW_HEADS // RW_GROUP
    gw = RW_GROUP * RW_HEAD
    tok = pl.BlockSpec((nrows, L, d), lambda i, j: (i, j, 0))
    vec = pl.BlockSpec((1, d), lambda i, j: (0, 0))
    st_spec = pl.BlockSpec((nrows, ngroups, gw, gw), lambda i, j: (i, 0, 0, 0))
    args = list(streams) + [r_k.reshape(1, d), ln_w.reshape(1, d), ln_b.reshape(1, d)]
    in_specs = [tok] * 7 + [vec] * 3
    if has_state:
        args.append(s0_bd)
        in_specs.append(st_spec)
    return pl.pallas_call(
        functools.partial(_rwkv_scan_body, L=L, has_state=has_state),
        grid=(b // nrows, t // L),
        in_specs=in_specs,
        out_specs=[tok, st_spec],
        out_shape=[jax.ShapeDtypeStruct((b, t, d), BF16),
                   jax.ShapeDtypeStruct((b, ngroups, gw, gw), F32)],
        scratch_shapes=[pltpu.VMEM((nrows, ngroups, gw, gw), F32)],
        compiler_params=_cparams("parallel", "arbitrary"),
        name="rwkv_scan",
    )(*args)


def _rwkv_state_to_bd(s):
    b = s.shape[0]
    g, m, hd = RW_HEADS // RW_GROUP, RW_GROUP, RW_HEAD
    s = s.reshape(b, g, m, hd, 1, hd) * jnp.eye(m, dtype=s.dtype)[None, None, :, None, :, None]
    return s.reshape(b, g, m * hd, m * hd)


def _rwkv_state_from_bd(sbd):
    b = sbd.shape[0]
    g, m, hd = RW_HEADS // RW_GROUP, RW_GROUP, RW_HEAD
    s = sbd.reshape(b, g, m, hd, m, hd)
    s = jnp.stack([s[:, :, i, :, i, :] for i in range(m)], axis=2)
    return s.reshape(b, RW_HEADS, hd, hd)


def _lru_body(*refs, tc, has_state):
    if has_state:
        (gate_ref, xb_ref, cw_ref, cb_ref, wg_ref, bg_ref, lam_ref, h0_ref, c0_ref,
         y_ref, hout_ref, xpad_ref, hc_ref) = refs
    else:
        (gate_ref, xb_ref, cw_ref, cb_ref, wg_ref, bg_ref, lam_ref,
         y_ref, hout_ref, xpad_ref, hc_ref) = refs
    t = pl.program_id(1)
    d = D_MODEL
    pad = 8

    @pl.when(t == 0)
    def _():
        xpad_ref[0:pad, :] = c0_ref[0] if has_state else jnp.zeros((pad, d), F32)
        hc_ref[...] = h0_ref[0] if has_state else jnp.zeros((1, d), F32)

    xpad_ref[pad:pad + tc, :] = xb_ref[0]
    xc = cb_ref[...]
    for j in range(CONV_W):
        off = pad - (CONV_W - 1) + j
        xc = xc + xpad_ref[off:off + tc, :] * cw_ref[j:j + 1, :]
    xpad_ref[0:pad, :] = xpad_ref[tc:tc + pad, :]

    xcb = xc.astype(BF16)
    r_pre, i_pre = [], []
    for nb in range(LRU_BLOCKS):
        sl = slice(nb * LRU_BW, (nb + 1) * LRU_BW)
        r_pre.append(_dot(xcb[:, sl], wg_ref[0, nb]))
        i_pre.append(_dot(xcb[:, sl], wg_ref[1, nb]))
    r_gate = _sigmoid(jnp.concatenate(r_pre, axis=-1) + bg_ref[0:1, :])
    i_gate = _sigmoid(jnp.concatenate(i_pre, axis=-1) + bg_ref[1:2, :])
    log_a = -LRU_C * r_gate * _softplus(-lam_ref[...])
    a = jnp.exp(log_a)
    om = -jnp.tanh(log_a) * (a * a + 1.0)
    bv = jnp.where(om > 0.0, om * lax.rsqrt(om), 0.0) * (i_gate * xc)

    grp = 8
    a = a.reshape(tc // grp, grp, d)
    bv = bv.reshape(tc // grp, grp, d)
    sub = _iota((1, grp, 1), 1)
    s = 1
    while s < grp:
        keep = sub >= s
        a_sh = jnp.where(keep, pltpu.roll(a, s, axis=1), 1.0)
        b_sh = jnp.where(keep, pltpu.roll(bv, s, axis=1), 0.0)
        bv = bv + a * b_sh
        a = a * a_sh
        s *= 2
    carry = hc_ref[...]
    rows = []
    for gi in range(tc // grp):
        h_g = bv[gi] + a[gi] * carry
        carry = h_g[grp - 1:grp, :]
        rows.append(h_g)
    hs = jnp.concatenate(rows, axis=0)
    last = carry
    hc_ref[...] = last
    hout_ref[0] = last
    y_ref[0] = (hs * jax.nn.gelu(gate_ref[0], approximate=True)).astype(BF16)


def lru_scan(proj, conv_w, conv_b, w_gates, b_gates, lam, h0, conv0, *, tc=256):
    b, t, _ = proj.shape
    d = D_MODEL
    tc = min(tc, t)
    has_state = h0 is not None
    full = lambda a: pl.BlockSpec(a.shape, lambda i, j, n=a.ndim: (0,) * n)
    consts = [conv_w, conv_b.reshape(1, d), w_gates, b_gates, lam.reshape(1, d)]
    args = [proj, proj] + consts
    in_specs = [pl.BlockSpec((1, tc, d), lambda i, j: (i, j, 0)),
                pl.BlockSpec((1, tc, d), lambda i, j: (i, j, 1))] + [full(a) for a in consts]
    if has_state:
        c0 = jnp.pad(conv0, ((0, 0), (8 - (CONV_W - 1), 0), (0, 0)))
        args += [h0.reshape(b, 1, d), c0]
        in_specs += [pl.BlockSpec((1, 1, d), lambda i, j: (i, 0, 0)),
                     pl.BlockSpec((1, 8, d), lambda i, j: (i, 0, 0))]
    y, hl = pl.pallas_call(
        functools.partial(_lru_body, tc=tc, has_state=has_state),
        grid=(b, t // tc),
        in_specs=in_specs,
        out_specs=[pl.BlockSpec((1, tc, d), lambda i, j: (i, j, 0)),
                   pl.BlockSpec((1, 1, d), lambda i, j: (i, 0, 0))],
        out_shape=[jax.ShapeDtypeStruct((b, t, d), BF16), jax.ShapeDtypeStruct((b, 1, d), F32)],
        scratch_shapes=[pltpu.VMEM((tc + 8, d), F32), pltpu.VMEM((1, d), F32)],
        compiler_params=_cparams("parallel", "arbitrary"),
        name="lru_scan",
    )(*args)
    return y, hl.reshape(b, d)


def _trunk(x, pos0, st, p):
    b, t, d = x.shape
    m = b * t
    get = (lambda name: st[name][0]) if st is not None else (lambda name: None)
    x2 = x.reshape(m, d)

    proj = norm_matmul(x2, p['norm_mix'][0], p['hg_w_in']).reshape(b, t, -1)
    o, hg_s = hgrn_scan(proj, p['hg_lb'], p['hg_norm'], get('hgrn'), layer=0)
    x2 = mix_out_ffn(o.reshape(m, d), p['hg_w_out'], x2, p['norm_ffn'][0], p['ffn_w_in'][0], p['ffn_w_out'][0],
                     p['norm_final'], final_norm=False)

    proj = norm_matmul(x2, p['norm_mix'][1], p['ret_w_in'], out_dtype=BF16).reshape(b, t, -1)
    cos, sin = rope_table(pos0, t)
    o, ret_s = ret_scan(proj, cos, sin, p['ret_norm'], get('ret'))
    x2 = mix_out_ffn(o.reshape(m, -1), p['ret_w_out'], x2, p['norm_ffn'][1], p['ffn_w_in'][1], p['ffn_w_out'][1],
                     p['norm_final'], final_norm=False)

    s0 = get('rwkv')
    streams, shift = rwkv_proj(x2.reshape(b, t, d), p['norm_mix'][2], get('shift'), p['rw'])
    o, rw_bd = rwkv_scan(streams, p['rw']['r_k'], p['rw']['ln_w'], p['rw']['ln_b'],
                         None if s0 is None else _rwkv_state_to_bd(s0))
    rw_s = _rwkv_state_from_bd(rw_bd)
    x2 = mix_out_ffn(o.reshape(m, d), p['rw']['w_out'], x2, p['norm_ffn'][2], p['ffn_w_in'][2], p['ffn_w_out'][2],
                     p['norm_final'], final_norm=False)

    proj = norm_matmul(x2, p['norm_mix'][3], p['lru_w_in']).reshape(b, t, -1)
    o, lru_s = lru_scan(proj, p['lru_conv_w'], p['lru_conv_b'], p['lru_w_gates'], p['lru_b_gates'],
                        p['lru_lambda'], get('lru'), get('conv'))
    if st is None:
        conv_s = proj[:, t - (CONV_W - 1):, d:]
    else:
        conv_s = jnp.concatenate([st['conv'][0], proj[:, :, d:]], axis=1)[:, -(CONV_W - 1):]
    y = mix_out_ffn(o.reshape(m, d), p['lru_w_out'], x2, p['norm_ffn'][3], p['ffn_w_in'][3], p['ffn_w_out'][3],
                    p['norm_final'], final_norm=True)
    return (y.reshape(b, t, d), hg_s[None], ret_s[None], rw_s[None], shift[None], lru_s[None], conv_s[None])


def kernel(x_prompt, x_sample, state_hgrn, state_ret, state_rwkv, state_rwkv_shift, state_lru, state_lru_conv, norm_mix, norm_ffn, norm_final, hg_lb, hg_w_in, hg_norm, hg_w_out, ret_w_in, ret_norm, ret_w_out, rw_mu, rw_w_rkv, rw_w0, rw_w1, rw_w2, rw_a0, rw_a1, rw_a2, rw_g1, rw_g2, rw_k_k, rw_k_a, rw_r_k, rw_ln_w, rw_ln_b, rw_w_out, lru_w_in, lru_conv_w, lru_conv_b, lru_w_gates, lru_b_gates, lru_lambda, lru_w_out, ffn_w_in, ffn_w_out):
    bf = lambda a: a.astype(BF16)
    p = dict(
        norm_mix=norm_mix, norm_ffn=norm_ffn, norm_final=norm_final,
        hg_lb=hg_lb, hg_w_in=bf(hg_w_in[0]), hg_norm=hg_norm[0], hg_w_out=bf(hg_w_out[0]),
        ret_w_in=bf(ret_w_in[0]), ret_norm=ret_norm[0], ret_w_out=bf(ret_w_out[0]),
        rw=dict(mu=rw_mu[0], w_rkv=bf(rw_w_rkv[0]), w0=rw_w0[0], w1=bf(rw_w1[0]), w2=bf(rw_w2[0]),
                a0=rw_a0[0], a1=bf(rw_a1[0]), a2=bf(rw_a2[0]), g1=bf(rw_g1[0]), g2=bf(rw_g2[0]),
                k_k=rw_k_k[0], k_a=rw_k_a[0], r_k=rw_r_k[0], ln_w=rw_ln_w[0], ln_b=rw_ln_b[0],
                w_out=bf(rw_w_out[0])),
        lru_w_in=bf(lru_w_in[0]), lru_conv_w=lru_conv_w[0], lru_conv_b=lru_conv_b[0],
        lru_w_gates=bf(lru_w_gates[0]), lru_b_gates=lru_b_gates[0], lru_lambda=lru_lambda[0],
        lru_w_out=bf(lru_w_out[0]),
        ffn_w_in=bf(ffn_w_in), ffn_w_out=bf(ffn_w_out),
    )
    yp, hg_p, ret_p, rw_p, sh_p, lru_p, conv_p = _trunk(x_prompt, 0, None, p)
    st = dict(hgrn=state_hgrn, ret=state_ret, rwkv=state_rwkv, shift=state_rwkv_shift,
              lru=state_lru, conv=state_lru_conv)
    ys, hg_s, ret_s, rw_s, sh_s, lru_s, conv_s = _trunk(x_sample, PAST_LEN, st, p)
    return (yp, ys, hg_p, hg_s, ret_p, ret_s, rw_p, rw_s, sh_p, sh_s, lru_p, lru_s, conv_p, conv_s)
```

```python
import functools
import math

import jax
import jax.numpy as jnp
from jax import lax
from jax.experimental import pallas as pl
from jax.experimental.pallas import tpu as pltpu

F32 = jnp.float32
BF16 = jnp.bfloat16

D_MODEL = 1024
NORM_EPS = 1e-6
HG_HEADS, HG_DK, HG_BLOCK = 8, 128, 16
HG_SUPER = 4
RET_HEADS, RET_DK, RET_DV = 4, 256, 512
ROPE_BASE = 10000.0
RW_HEADS, RW_HEAD = 16, 64
RW_GROUP = 4
RW_GN_EPS = 64e-5
LRU_BLOCKS, LRU_BW, CONV_W, LRU_C = 8, 128, 4, 8.0
D_FF = 2816
PAST_LEN = 2048

VMEM_LIMIT_BYTES = 56 * 1024 * 1024


def _cparams(*sem):
    return pltpu.CompilerParams(dimension_semantics=sem, vmem_limit_bytes=VMEM_LIMIT_BYTES)


def _const_spec(shape):
    n = len(shape)
    return pl.BlockSpec(tuple(shape), lambda *_: (0,) * n, pipeline_mode=pl.Buffered(1))


def _iota(shape, dim):
    return lax.broadcasted_iota(jnp.int32, shape, dim)


def _dot(a, b):
    return jnp.dot(a, b, preferred_element_type=F32)


def _dot_nt(a, b):
    return lax.dot_general(a, b, (((1,), (1,)), ((), ())), preferred_element_type=F32)


def _dot_tn(a, b):
    return lax.dot_general(a, b, (((0,), (0,)), ((), ())), preferred_element_type=F32)


def _split3(x):
    hi = x.astype(BF16)
    r1 = x - hi.astype(F32)
    mid = r1.astype(BF16)
    lo = (r1 - mid.astype(F32)).astype(BF16)
    return hi, mid, lo


def _sel_dot(m01, x, pieces=3):
    return sum(_dot(m01, p) for p in _split3(x)[:pieces])


def _dot_sel(x, m01, pieces=3):
    return sum(_dot(p, m01) for p in _split3(x)[:pieces])


def _rms(x, w):
    return x * lax.rsqrt(jnp.mean(x * x, axis=-1, keepdims=True) + NORM_EPS) * w


def _sigmoid(x):
    return 0.5 * jnp.tanh(0.5 * x) + 0.5


def _softplus(x):
    return jnp.maximum(x, 0.0) + jnp.log1p(jnp.exp(-jnp.abs(x)))


def _log2(n):
    l = int(math.log2(n))
    assert 1 << l == n, n
    return l


def _norm_mm_body(x_ref, nw_ref, w_ref, o_ref, *, tn):
    h = _rms(x_ref[...], nw_ref[...]).astype(BF16)
    for c in range(w_ref.shape[1] // tn):
        o_ref[:, c * tn:(c + 1) * tn] = _dot(h, w_ref[:, c * tn:(c + 1) * tn]).astype(o_ref.dtype)


def norm_matmul(x2d, nw, w, *, out_dtype=F32, tm=512, tn=1024):
    m, k = x2d.shape
    n = w.shape[1]
    tm, tn = min(tm, m), min(tn, n)
    return pl.pallas_call(
        functools.partial(_norm_mm_body, tn=tn),
        grid=(m // tm,),
        in_specs=[pl.BlockSpec((tm, k), lambda i: (i, 0)),
                  _const_spec((1, k)),
                  _const_spec((k, n))],
        out_specs=pl.BlockSpec((tm, n), lambda i: (i, 0)),
        out_shape=jax.ShapeDtypeStruct((m, n), out_dtype),
        compiler_params=_cparams("parallel"),
        name="norm_matmul",
    )(x2d, nw.reshape(1, k), w)


def _mix_ffn_body(o_ref, wo_ref, x_ref, nw_ref, win_ref, wout_ref, fnw_ref, y_ref, *, final_norm):
    x1 = x_ref[...] + _dot(o_ref[...].astype(BF16), wo_ref[...])
    h = _rms(x1, nw_ref[...]).astype(BF16)
    g = _dot(h, win_ref[:, 0:D_FF])
    u = _dot(h, win_ref[:, D_FF:2 * D_FF])
    act = (g * _sigmoid(g) * u).astype(BF16)
    acc = x1 + _dot(act, wout_ref[...])
    if final_norm:
        acc = _rms(acc, fnw_ref[...])
    y_ref[...] = acc


def mix_out_ffn(o2d, w_o, x2d, nw, w_in, w_out, fnw, *, final_norm, tm=512):
    m, ko = o2d.shape
    d = x2d.shape[1]
    tm = min(tm, m)
    return pl.pallas_call(
        functools.partial(_mix_ffn_body, final_norm=final_norm),
        grid=(m // tm,),
        in_specs=[pl.BlockSpec((tm, ko), lambda i: (i, 0)),
                  _const_spec((ko, d)),
                  pl.BlockSpec((tm, d), lambda i: (i, 0)),
                  _const_spec((1, d)),
                  _const_spec((d, 2 * D_FF)),
                  _const_spec((D_FF, d)),
                  _const_spec((1, d))],
        out_specs=pl.BlockSpec((tm, d), lambda i: (i, 0)),
        out_shape=jax.ShapeDtypeStruct((m, d), F32),
        compiler_params=_cparams("parallel"),
        name="mix_out_ffn",
    )(o2d, w_o, x2d, nw.reshape(1, d), w_in, w_out, fnw.reshape(1, d))


def _hgrn_body(*refs, tc, layer, has_state):
    if has_state:
        q_ref, f_ref, v_ref, g_ref, lbp_ref, gn_ref, s0_ref, o_ref, sout_ref, st_ref = refs
    else:
        q_ref, f_ref, v_ref, g_ref, lbp_ref, gn_ref, o_ref, sout_ref, st_ref = refs
    t = pl.program_id(1)
    blk = min(HG_BLOCK, tc)
    heads, dk = HG_HEADS, HG_DK
    nb = tc // blk
    m = min(HG_SUPER, nb)
    sbr = m * blk

    nrows = q_ref.shape[0]

    @pl.when(t == 0)
    def _():
        for rb in range(nrows):
            for h in range(heads):
                st_ref[rb, h] = s0_ref[rb, h].T if has_state else jnp.zeros((dk, dk), F32)

    lbp = lbp_ref[...]
    e = jnp.exp(lbp - jnp.max(lbp, axis=0, keepdims=True))
    lb = jnp.sum(e[:layer + 1], axis=0, keepdims=True) / jnp.sum(e, axis=0, keepdims=True)

    sh = _log2(blk)
    ri, ci = _iota((tc, tc), 0), _iota((tc, tc), 1)
    bdiff = (ri >> sh) - (ci >> sh)
    same_sb = (ri >> _log2(sbr)) == (ci >> _log2(sbr))
    causal = (bdiff == 0) & (ci <= ri)
    causal16, same16 = causal.astype(BF16), (bdiff == 0).astype(BF16)
    masks = [causal] + [same_sb & (bdiff == dd) for dd in range(1, m)]
    rows_of = lambda b: slice(b * blk, (b + 1) * blk)
    zero_blk = jnp.zeros((blk, heads * dk), F32)

    def prep(rb):
        half_th = 0.5 * jnp.tanh(0.5 * f_ref[rb])
        f = lb + (1.0 - lb) * (0.5 + half_th)
        k = (1.0 - lb) * (0.5 - half_th)
        lf = jnp.log(f)
        cum = _sel_dot(causal16, lf, pieces=2)
        tot = _sel_dot(same16, lf, pieces=2)
        q_in = q_ref[rb] * jnp.exp(cum)
        k_in = k * jnp.exp(-cum)
        kst1 = k * jnp.exp(tot - cum)
        drow = [jnp.exp(tot[b * blk:b * blk + 1, :]) for b in range(nb)]
        kst = {1: [kst1[rows_of(b)] for b in range(nb)]}
        for dd in range(2, m + 1):
            kst[dd] = [kst[dd - 1][b] * drow[b + dd - 1] if (b % m) + dd - 1 <= m - 1 else None for b in range(nb)]
        k_far = [jnp.concatenate([x if x is not None else zero_blk for x in kst[dd]], axis=0) for dd in range(2, m)]
        k_end = jnp.concatenate([kst[m - (b % m)][b] for b in range(nb)], axis=0)
        q_sb = []
        for b in range(nb):
            x = q_in[rows_of(b)]
            for l in range(1, (b % m) + 1):
                x = x * drow[b - l]
            q_sb.append(x)
        dec_sb = []
        for sbi in range(nb // m):
            x = drow[sbi * m]
            for b in range(sbi * m + 1, (sbi + 1) * m):
                x = x * drow[b]
            dec_sb.append(x)
        return dict(q_in=q_in, k_all=[k_in, kst1] + k_far, k_end=k_end, q_sb=jnp.concatenate(q_sb, axis=0),
                    dec_sb=dec_sb, v=v_ref[rb])

    rowd = [prep(rb) for rb in range(nrows)]
    units = [(rb, h) for rb in range(nrows) for h in range(heads)]
    sls = [slice(h * dk, (h + 1) * dk) for _, h in units]
    col = lambda name: [rowd[rb][name][:, sl] for (rb, _), sl in zip(units, sls)]
    q_in, k_end, q_sb, v = col('q_in'), col('k_end'), col('q_sb'), col('v')
    rhs = [jnp.concatenate([x[:, sl] for x in rowd[rb]['k_all']], axis=0).astype(BF16) for (rb, _), sl in zip(units, sls)]
    sc = [_dot_nt(q_in[u].astype(BF16), rhs[u]) for u in range(len(units))]
    sc = [sum(jnp.where(masks[dd], x[:, dd * tc:(dd + 1) * tc], 0.0) for dd in range(m)).astype(BF16) for x in sc]
    o = [_dot(sc[u], v[u].astype(BF16)) for u in range(len(units))]
    st = [st_ref[rb, h] for rb, h in units]
    o_in = [[] for _ in units]
    for sbi in range(nb // m):
        rows = slice(sbi * sbr, (sbi + 1) * sbr)
        for u in range(len(units)):
            o_in[u].append(_dot_nt(q_sb[u][rows].astype(BF16), st[u].astype(BF16)))
        kv = [_dot_tn(v[u][rows].astype(BF16), k_end[u][rows].astype(BF16)) for u in range(len(units))]
        st = [st[u] * rowd[rb]['dec_sb'][sbi][:, sls[u]] + kv[u] for u, (rb, _) in enumerate(units)]
    for u, (rb, h) in enumerate(units):
        st_ref[rb, h] = st[u]

    gn = gn_ref[...]
    for u, (rb, h) in enumerate(units):
        sl = sls[u]
        oh = o[u] + jnp.concatenate(o_in[u], axis=0)
        oh = oh * lax.rsqrt(jnp.mean(oh * oh, axis=-1, keepdims=True) + NORM_EPS)
        gh = g_ref[rb, :, sl]
        o_ref[rb, :, sl] = (oh * gn[:, sl] * (gh * _sigmoid(gh))).astype(BF16)

    @pl.when(t == pl.num_programs(1) - 1)
    def _():
        for rb in range(nrows):
            for h in range(heads):
                sout_ref[rb, h] = st_ref[rb, h].T


def hgrn_scan(proj, hg_lb, g_norm, s0, *, layer, tc=128, rows=2):
    b, t, _ = proj.shape
    d = D_MODEL
    tc = min(tc, t)
    nrows = rows if b % rows == 0 else 1
    has_state = s0 is not None
    col = lambda c: pl.BlockSpec((nrows, tc, d), lambda i, j, c=c: (i, j, c))
    in_specs = [col(0), col(1), col(2), col(3),
                pl.BlockSpec(hg_lb.shape, lambda i, j: (0, 0)),
                pl.BlockSpec((1, d), lambda i, j: (0, 0))]
    args = [proj, proj, proj, proj, hg_lb, g_norm.reshape(1, d)]
    st_spec = pl.BlockSpec((nrows, HG_HEADS, HG_DK, HG_DK), lambda i, j: (i, 0, 0, 0))
    if has_state:
        in_specs.append(st_spec)
        args.append(s0)
    return pl.pallas_call(
        functools.partial(_hgrn_body, tc=tc, layer=layer, has_state=has_state),
        grid=(b // nrows, t // tc),
        in_specs=in_specs,
        out_specs=[pl.BlockSpec((nrows, tc, d), lambda i, j: (i, j, 0)), st_spec],
        out_shape=[jax.ShapeDtypeStruct((b, t, d), BF16),
                   jax.ShapeDtypeStruct((b, HG_HEADS, HG_DK, HG_DK), F32)],
        scratch_shapes=[pltpu.VMEM((nrows, HG_HEADS, HG_DK, HG_DK), F32)],
        compiler_params=_cparams("parallel", "arbitrary"),
        name="hgrn_scan",
    )(*args)


def _rope_body(cos_ref, sin_ref, *, pos0, tc):
    half = RET_DK // 2
    pos = (pos0 + pl.program_id(0) * tc + _iota((tc, half), 0)).astype(F32)
    inv = jnp.power(jnp.float32(ROPE_BASE), -(_iota((tc, half), 1).astype(F32) / half))
    ang = pos * inv
    cos_ref[...] = jnp.cos(ang)
    sin_ref[...] = jnp.sin(ang)


def rope_table(pos0, t, *, tc=256):
    tc = min(tc, t)
    half = RET_DK // 2
    return pl.pallas_call(
        functools.partial(_rope_body, pos0=pos0, tc=tc),
        grid=(t // tc,),
        out_specs=[pl.BlockSpec((tc, half), lambda i: (i, 0))] * 2,
        out_shape=[jax.ShapeDtypeStruct((t, half), F32)] * 2,
        compiler_params=_cparams("parallel"),
        name="rope_table",
    )()


def _ret_body(*refs, tc, has_state):
    if has_state:
        q_ref, k_ref, v_ref, g_ref, cos_ref, sin_ref, gn_ref, s0_ref, o_ref, sout_ref, s_ref, dmat_ref = refs
    else:
        q_ref, k_ref, v_ref, g_ref, cos_ref, sin_ref, gn_ref, o_ref, sout_ref, s_ref, dmat_ref = refs
    t = pl.program_id(1)
    nh, dk, dv = RET_HEADS, RET_DK, RET_DV
    half = dk // 2
    heads = range(nh)
    lg = [math.log1p(-2.0 ** (-5.0 - h)) for h in heads]

    @pl.when(t == 0)
    def _():
        s_ref[...] = s0_ref[0] if has_state else jnp.zeros((nh, dk, dv), F32)
        diff = (_iota((tc, tc), 0) - _iota((tc, tc), 1)).astype(F32)
        for h in heads:
            dmat_ref[h] = jnp.where(diff >= 0.0, jnp.exp(diff * lg[h]), 0.0)

    cos, sin = cos_ref[...], sin_ref[...]

    def rot(x):
        x1, x2 = x[:, :half], x[:, half:]
        return jnp.concatenate([x1 * cos - x2 * sin, x1 * sin + x2 * cos], axis=-1)

    ksl = [slice(h * dk, (h + 1) * dk) for h in heads]
    vsl = [slice(h * dv, (h + 1) * dv) for h in heads]
    row = _iota((tc, 1), 0).astype(F32)
    q = [rot(q_ref[0, :, sl].astype(F32)) for sl in ksl]
    k = [rot(k_ref[0, :, sl].astype(F32)) * (dk ** -0.5) for sl in ksl]
    v = [v_ref[0, :, sl].astype(BF16) for sl in vsl]
    s = [s_ref[h] for h in heads]
    sc = [_dot_nt(q[h].astype(BF16), k[h].astype(BF16)) for h in heads]
    sc = [(dmat_ref[h] * sc[h]).astype(BF16) for h in heads]
    o = [_dot(sc[h], v[h]) + _dot((q[h] * jnp.exp((row + 1.0) * lg[h])).astype(BF16), s[h].astype(BF16))
         for h in heads]
    kv = [_dot_tn((k[h] * jnp.exp((tc - 1.0 - row) * lg[h])).astype(BF16), v[h]) for h in heads]
    for h in heads:
        s_ref[h] = s[h] * math.exp(tc * lg[h]) + kv[h]
        oh = o[h] * lax.rsqrt(jnp.mean(o[h] * o[h], axis=-1, keepdims=True) + NORM_EPS)
        g = g_ref[0, :, vsl[h]].astype(F32)
        o_ref[0, :, vsl[h]] = (oh * gn_ref[:, vsl[h]] * (g * _sigmoid(g))).astype(BF16)

    @pl.when(t == pl.num_programs(1) - 1)
    def _():
        sout_ref[0] = s_ref[...]


def ret_scan(proj, cos, sin, g_norm, s0, *, tc=256):
    b, t, _ = proj.shape
    tc = min(tc, t)
    has_state = s0 is not None
    nh, dk, dv = RET_HEADS, RET_DK, RET_DV
    half = dk // 2
    in_specs = [pl.BlockSpec((1, tc, nh * dk), lambda i, j: (i, j, 0)),
                pl.BlockSpec((1, tc, nh * dk), lambda i, j: (i, j, 1)),
                pl.BlockSpec((1, tc, nh * dv), lambda i, j: (i, j, 1)),
                pl.BlockSpec((1, tc, nh * dv), lambda i, j: (i, j, 2)),
                pl.BlockSpec((tc, half), lambda i, j: (j, 0)),
                pl.BlockSpec((tc, half), lambda i, j: (j, 0)),
                _const_spec((1, nh * dv))]
    args = [proj, proj, proj, proj, cos, sin, g_norm.reshape(1, nh * dv)]
    st_spec = pl.BlockSpec((1, nh, dk, dv), lambda i, j: (i, 0, 0, 0))
    if has_state:
        in_specs.append(st_spec)
        args.append(s0)
    return pl.pallas_call(
        functools.partial(_ret_body, tc=tc, has_state=has_state),
        grid=(b, t // tc),
        in_specs=in_specs,
        out_specs=[pl.BlockSpec((1, tc, nh * dv), lambda i, j: (i, j, 0)), st_spec],
        out_shape=[jax.ShapeDtypeStruct((b, t, nh * dv), BF16),
                   jax.ShapeDtypeStruct((b, nh, dk, dv), F32)],
        scratch_shapes=[pltpu.VMEM((nh, dk, dv), F32), pltpu.VMEM((nh, tc, tc), F32)],
        compiler_params=_cparams("parallel", "arbitrary"),
        name="ret_scan",
    )(*args)


def _rwkv_proj_body(*refs, has_state):
    if has_state:
        (x_ref, nw_ref, sh0_ref, mu_ref, wrkv_ref, w0_ref, w1_ref, w2_ref, a0_ref, a1_ref, a2_ref,
         g1_ref, g2_ref, kk_ref, ka_ref, hsum_ref, hexp_ref,
         r_o, k_o, v_o, kk_o, a_o, lw_o, g_o, sh_o, carry_ref) = refs
    else:
        (x_ref, nw_ref, mu_ref, wrkv_ref, w0_ref, w1_ref, w2_ref, a0_ref, a1_ref, a2_ref,
         g1_ref, g2_ref, kk_ref, ka_ref, hsum_ref, hexp_ref,
         r_o, k_o, v_o, kk_o, a_o, lw_o, g_o, sh_o, carry_ref) = refs
    t = pl.program_id(1)
    tm = x_ref.shape[1]

    @pl.when(t == 0)
    def _():
        carry_ref[...] = sh0_ref[0] if has_state else jnp.zeros((1, D_MODEL), F32)

    h = _rms(x_ref[0], nw_ref[...])
    prev = jnp.where(_iota((tm, 1), 0) == 0, carry_ref[...], pltpu.roll(h, 1, axis=0))
    last = h[tm - 1:tm, :]
    carry_ref[...] = last
    sh_o[0] = last
    d = prev - h
    mix = lambda i: (h + d * mu_ref[i:i + 1, :]).astype(BF16)
    r = _dot(mix(0), wrkv_ref[0])
    k = _dot(mix(1), wrkv_ref[1])
    v = _dot(mix(2), wrkv_ref[2])
    w_pre = w0_ref[...] + _dot(jnp.tanh(_dot(mix(3), w1_ref[...])).astype(BF16), w2_ref[...])
    lw_o[0] = -math.exp(-0.5) * _sigmoid(w_pre)
    a = _sigmoid(a0_ref[...] + _dot(_dot(mix(4), a1_ref[...]).astype(BF16), a2_ref[...]))
    g_o[0] = _dot(_sigmoid(_dot(mix(5), g1_ref[...])).astype(BF16), g2_ref[...]).astype(g_o.dtype)
    kk = k * kk_ref[...]
    ss = _dot_sel(kk * kk, hsum_ref[...], pieces=2)
    inv = 1.0 / jnp.maximum(jnp.sqrt(ss), 1e-12)
    kk_o[0] = (kk * _dot_sel(inv, hexp_ref[...], pieces=2)).astype(kk_o.dtype)
    r_o[0] = r.astype(r_o.dtype)
    k_o[0] = (k * (1.0 + (a - 1.0) * ka_ref[...])).astype(k_o.dtype)
    v_o[0] = v.astype(v_o.dtype)
    a_o[0] = a


def rwkv_proj(x, nw, shift0, p, *, tm=256):
    b, t, d = x.shape
    tm = min(tm, t)
    has_state = shift0 is not None
    row = lambda a: a.reshape(1, d)
    head_of_lane = jnp.arange(d) // RW_HEAD
    hsum = (head_of_lane[:, None] == jnp.arange(RW_HEADS)[None, :]).astype(BF16)
    hexp = hsum.T
    full = lambda a: pl.BlockSpec(a.shape, lambda i, j, n=a.ndim: (0,) * n)
    tok = pl.BlockSpec((1, tm, d), lambda i, j: (i, j, 0))
    args = [x, row(nw)]
    in_specs = [tok, full(row(nw))]
    if has_state:
        args.append(shift0.reshape(b, 1, d))
        in_specs.append(pl.BlockSpec((1, 1, d), lambda i, j: (i, 0, 0)))
    consts = [p['mu'], p['w_rkv'], row(p['w0']), p['w1'], p['w2'], row(p['a0']), p['a1'], p['a2'],
              p['g1'], p['g2'], row(p['k_k']), row(p['k_a']), hsum, hexp]
    args += consts
    in_specs += [full(a) for a in consts]
    outs = pl.pallas_call(
        functools.partial(_rwkv_proj_body, has_state=has_state),
        grid=(b, t // tm),
        in_specs=in_specs,
        out_specs=[tok] * 7 + [pl.BlockSpec((1, 1, d), lambda i, j: (i, 0, 0))],
        out_shape=[jax.ShapeDtypeStruct((b, t, d), dt) for dt in (BF16, BF16, BF16, BF16, F32, F32, BF16)]
        + [jax.ShapeDtypeStruct((b, 1, d), F32)],
        scratch_shapes=[pltpu.VMEM((1, d), F32)],
        compiler_params=_cparams("parallel", "arbitrary"),
        name="rwkv_proj",
    )(*args)
    return outs[:7], outs[7].reshape(b, d)


def _rwkv_scan_body(*refs, L, has_state):
    if has_state:
        (r_ref, k_ref, v_ref, kk_ref, a_ref, lw_ref, g_ref, rk_ref, lnw_ref, lnb_ref, s0_ref,
         o_ref, sout_ref, s_ref) = refs
    else:
        (r_ref, k_ref, v_ref, kk_ref, a_ref, lw_ref, g_ref, rk_ref, lnw_ref, lnb_ref,
         o_ref, sout_ref, s_ref) = refs
    t = pl.program_id(1)
    gw = RW_GROUP * RW_HEAD
    ngroups = RW_HEADS // RW_GROUP
    sl_l, sl_h = _log2(L), _log2(RW_HEAD)
    n = RW_GROUP * L

    @pl.when(t == 0)
    def _():
        s_ref[...] = s0_ref[...] if has_state else jnp.zeros(s_ref.shape, F32)

    nrows = lw_ref.shape[0]
    tri = (_iota((L, L), 1) <= _iota((L, L), 0)).astype(BF16)
    v_r, a_hat_r, r_hat_r, b_hat_r, k_hat_r, b_til_r, k_til_r, gam_l_r, bonus_r = ([] for _ in range(9))
    for rb in range(nrows):
        lw = lw_ref[rb]
        cum = _sel_dot(tri, lw)
        cum_l = cum[L - 1:L, :]
        e_pos, e_neg = jnp.exp(cum), jnp.exp(-cum)
        e_tail = jnp.exp(cum_l - cum)
        kk, a, r, k = kk_ref[rb].astype(F32), a_ref[rb], r_ref[rb].astype(F32), k_ref[rb].astype(F32)
        beta = kk * a
        v_r.append(v_ref[rb].astype(F32))
        a_hat_r.append(-kk * jnp.exp(cum - lw))
        r_hat_r.append(r * e_pos)
        b_hat_r.append(beta * e_neg)
        k_hat_r.append(k * e_neg)
        b_til_r.append(beta * e_tail)
        k_til_r.append(k * e_tail)
        gam_l_r.append(jnp.exp(cum_l))
        bonus_r.append(r * k * rk_ref[...])

    own = (_iota((n, 1), 0) >> sl_l) == (_iota((1, gw), 1) >> sl_h)
    tile = lambda x: jnp.concatenate([x] * RW_GROUP, axis=0)
    stack = lambda x: jnp.where(own, tile(x), 0.0).astype(BF16)
    same = (_iota((n, n), 0) >> sl_l) == (_iota((n, n), 1) >> sl_l)
    to_bd = lambda x: jnp.where(same, tile(x), 0.0).astype(BF16)
    row_c, col_c = _iota((L, n), 0), _iota((L, n), 1) & (L - 1)
    strict_c, incl_c = col_c < row_c, col_c <= row_c
    eye_c = (col_c == row_c).astype(F32)
    bdg = (_iota((gw, gw), 0) >> sl_h) == (_iota((gw, gw), 1) >> sl_h)
    bdg16 = bdg.astype(BF16)
    inv_n = 1.0 / RW_HEAD

    units = [(rb, gi) for rb in range(nrows) for gi in range(ngroups)]
    groups = range(len(units))
    sls = [slice(gi * gw, (gi + 1) * gw) for _, gi in units]
    pick = lambda per_row: [per_row[rb][:, sl] for (rb, _), sl in zip(units, sls)]
    v_u, b_til_u, k_til_u, bonus_u = pick(v_r), pick(b_til_r), pick(k_til_r), pick(bonus_r)
    gam_l_u = pick(gam_l_r)
    ar = [jnp.concatenate([x, y], axis=0).astype(BF16) for x, y in zip(pick(a_hat_r), pick(r_hat_r))]
    c = [_dot_nt(ar[u], jnp.concatenate([stack(x), stack(y)], axis=0))
         for u, (x, y) in enumerate(zip(pick(b_hat_r), pick(k_hat_r)))]
    p_c = [jnp.where(strict_c, x[:L, :n], 0.0) for x in c]
    m_k = [jnp.concatenate([jnp.where(strict_c, x[:L, n:], 0.0), jnp.where(incl_c, x[L:, n:], 0.0)],
                           axis=0).astype(BF16) for x in c]
    m_rb = [jnp.where(incl_c, x[L:, :n], 0.0).astype(BF16) for x in c]
    t_c = [eye_c + x for x in p_c]
    for j in range(sl_l):
        p_bd = [to_bd(x) for x in p_c]
        if j == 0:
            p_c = [_dot(p_c[gi].astype(BF16), p_bd[gi]) for gi in groups]
        elif j < sl_l - 1:
            both = [_dot(jnp.concatenate([p_c[gi], t_c[gi]], axis=0).astype(BF16), p_bd[gi]) for gi in groups]
            p_c = [x[:L] for x in both]
            t_c = [t_c[gi] + both[gi][L:] for gi in groups]
        else:
            t_c = [t_c[gi] + _dot(t_c[gi].astype(BF16), p_bd[gi]) for gi in groups]
    s_g = [s_ref[rb, gi] for rb, gi in units]
    q = [_dot_nt(ar[u], s_g[u].astype(BF16)) for u in groups]
    kv = [_dot(m_k[u], stack(v_u[u])) for u in groups]
    u_c = [_dot(t_c[u].astype(BF16), stack(q[u][:L] + kv[u][:L])) for u in groups]
    y = [q[u][L:] + kv[u][L:] + _dot(m_rb[u], stack(u_c[u])) for u in groups]
    for u, (rb, gi) in enumerate(units):
        upd = _dot_tn(jnp.concatenate([u_c[u], v_u[u]], axis=0).astype(BF16),
                      jnp.concatenate([b_til_u[u], k_til_u[u]], axis=0).astype(BF16))
        s_ref[rb, gi] = s_g[u] * gam_l_u[u] + jnp.where(bdg, upd, 0.0)
    pieces = []
    for u in groups:
        pieces += _split3(y[u]) + _split3(y[u] * y[u]) + _split3(bonus_u[u])
    sums = _dot(jnp.concatenate(pieces, axis=0), bdg16)
    for u, (rb, gi) in enumerate(units):
        sl = sls[u]
        sum3 = lambda i, base=9 * u * L: sums[base + 3 * i * L:base + (3 * i + 1) * L] \
            + sums[base + (3 * i + 1) * L:base + (3 * i + 2) * L] + sums[base + (3 * i + 2) * L:base + (3 * i + 3) * L]
        mu = sum3(0) * inv_n
        var = sum3(1) * inv_n - mu * mu
        yn = (y[u] - mu) * lax.rsqrt(var + RW_GN_EPS) * lnw_ref[:, sl] + lnb_ref[:, sl]
        o_ref[rb, :, sl] = ((yn + sum3(2) * v_u[u]) * g_ref[rb, :, sl].astype(F32)).astype(BF16)

    @pl.when(t == pl.num_programs(1) - 1)
    def _():
        hd = RW_HEAD
        for rb in range(nrows):
            for gi in range(ngroups):
                s_bd = s_ref[rb, gi]
                for i in range(RW_GROUP):
                    sout_ref[rb, RW_GROUP * gi + i] = s_bd[i * hd:(i + 1) * hd, i * hd:(i + 1) * hd]


def rwkv_scan(streams, r_k, ln_w, ln_b, s0_bd, *, chunk=64, rows=2):
    b, t, d = streams[0].shape
    L = min(chunk, t)
    nrows = rows if b % rows == 0 else 1
    has_state = s0_bd is not None
    ngroups = RW_HEADS // RW_GROUP
    gw = RW_GROUP * RW_HEAD
    tok = pl.BlockSpec((nrows, L, d), lambda i, j: (i, j, 0))
    vec = pl.BlockSpec((1, d), lambda i, j: (0, 0))
    st_spec = pl.BlockSpec((nrows, ngroups, gw, gw), lambda i, j: (i, 0, 0, 0))
    args = list(streams) + [r_k.reshape(1, d), ln_w.reshape(1, d), ln_b.reshape(1, d)]
    in_specs = [tok] * 7 + [vec] * 3
    if has_state:
        args.append(s0_bd)
        in_specs.append(st_spec)
    return pl.pallas_call(
        functools.partial(_rwkv_scan_body, L=L, has_state=has_state),
        grid=(b // nrows, t // L),
        in_specs=in_specs,
        out_specs=[tok, pl.BlockSpec((nrows, RW_HEADS, RW_HEAD, RW_HEAD), lambda i, j: (i, 0, 0, 0))],
        out_shape=[jax.ShapeDtypeStruct((b, t, d), BF16),
                   jax.ShapeDtypeStruct((b, RW_HEADS, RW_HEAD, RW_HEAD), F32)],
        scratch_shapes=[pltpu.VMEM((nrows, ngroups, gw, gw), F32)],
        compiler_params=_cparams("parallel", "arbitrary"),
        name="rwkv_scan",
    )(*args)


def _rwkv_state_to_bd(s):
    b = s.shape[0]
    g, m, hd = RW_HEADS // RW_GROUP, RW_GROUP, RW_HEAD
    s = s.reshape(b, g, m, hd, 1, hd) * jnp.eye(m, dtype=s.dtype)[None, None, :, None, :, None]
    return s.reshape(b, g, m * hd, m * hd)


def _lru_body(*refs, tc, has_state):
    if has_state:
        (gate_ref, xb_ref, cw_ref, cb_ref, wg_ref, bg_ref, lam_ref, h0_ref, c0_ref,
         y_ref, hout_ref, xpad_ref, hc_ref) = refs
    else:
        (gate_ref, xb_ref, cw_ref, cb_ref, wg_ref, bg_ref, lam_ref,
         y_ref, hout_ref, xpad_ref, hc_ref) = refs
    t = pl.program_id(1)
    d = D_MODEL
    pad = 8

    @pl.when(t == 0)
    def _():
        xpad_ref[0:pad, :] = c0_ref[0] if has_state else jnp.zeros((pad, d), F32)
        hc_ref[...] = h0_ref[0] if has_state else jnp.zeros((1, d), F32)

    xpad_ref[pad:pad + tc, :] = xb_ref[0]
    xc = cb_ref[...]
    for j in range(CONV_W):
        off = pad - (CONV_W - 1) + j
        xc = xc + xpad_ref[off:off + tc, :] * cw_ref[j:j + 1, :]
    xpad_ref[0:pad, :] = xpad_ref[tc:tc + pad, :]

    xcb = xc.astype(BF16)
    r_pre, i_pre = [], []
    for nb in range(LRU_BLOCKS):
        sl = slice(nb * LRU_BW, (nb + 1) * LRU_BW)
        r_pre.append(_dot(xcb[:, sl], wg_ref[0, nb]))
        i_pre.append(_dot(xcb[:, sl], wg_ref[1, nb]))
    r_gate = _sigmoid(jnp.concatenate(r_pre, axis=-1) + bg_ref[0:1, :])
    i_gate = _sigmoid(jnp.concatenate(i_pre, axis=-1) + bg_ref[1:2, :])
    log_a = -LRU_C * r_gate * _softplus(-lam_ref[...])
    a = jnp.exp(log_a)
    om = -jnp.tanh(log_a) * (a * a + 1.0)
    bv = jnp.where(om > 0.0, om * lax.rsqrt(om), 0.0) * (i_gate * xc)

    grp = 8
    a = a.reshape(tc // grp, grp, d)
    bv = bv.reshape(tc // grp, grp, d)
    sub = _iota((1, grp, 1), 1)
    s = 1
    while s < grp:
        keep = sub >= s
        a_sh = jnp.where(keep, pltpu.roll(a, s, axis=1), 1.0)
        b_sh = jnp.where(keep, pltpu.roll(bv, s, axis=1), 0.0)
        bv = bv + a * b_sh
        a = a * a_sh
        s *= 2
    carry = hc_ref[...]
    rows = []
    for gi in range(tc // grp):
        h_g = bv[gi] + a[gi] * carry
        carry = h_g[grp - 1:grp, :]
        rows.append(h_g)
    hs = jnp.concatenate(rows, axis=0)
    last = carry
    hc_ref[...] = last
    hout_ref[0] = last
    y_ref[0] = (hs * jax.nn.gelu(gate_ref[0], approximate=True)).astype(BF16)


def lru_scan(proj, conv_w, conv_b, w_gates, b_gates, lam, h0, conv0, *, tc=256):
    b, t, _ = proj.shape
    d = D_MODEL
    tc = min(tc, t)
    has_state = h0 is not None
    full = lambda a: pl.BlockSpec(a.shape, lambda i, j, n=a.ndim: (0,) * n)
    consts = [conv_w, conv_b.reshape(1, d), w_gates, b_gates, lam.reshape(1, d)]
    args = [proj, proj] + consts
    in_specs = [pl.BlockSpec((1, tc, d), lambda i, j: (i, j, 0)),
                pl.BlockSpec((1, tc, d), lambda i, j: (i, j, 1))] + [full(a) for a in consts]
    if has_state:
        c0 = jnp.pad(conv0, ((0, 0), (8 - (CONV_W - 1), 0), (0, 0)))
        args += [h0.reshape(b, 1, d), c0]
        in_specs += [pl.BlockSpec((1, 1, d), lambda i, j: (i, 0, 0)),
                     pl.BlockSpec((1, 8, d), lambda i, j: (i, 0, 0))]
    y, hl = pl.pallas_call(
        functools.partial(_lru_body, tc=tc, has_state=has_state),
        grid=(b, t // tc),
        in_specs=in_specs,
        out_specs=[pl.BlockSpec((1, tc, d), lambda i, j: (i, j, 0)),
                   pl.BlockSpec((1, 1, d), lambda i, j: (i, 0, 0))],
        out_shape=[jax.ShapeDtypeStruct((b, t, d), BF16), jax.ShapeDtypeStruct((b, 1, d), F32)],
        scratch_shapes=[pltpu.VMEM((tc + 8, d), F32), pltpu.VMEM((1, d), F32)],
        compiler_params=_cparams("parallel", "arbitrary"),
        name="lru_scan",
    )(*args)
    return y, hl.reshape(b, d)


def _trunk(x, pos0, st, p):
    b, t, d = x.shape
    m = b * t
    get = (lambda name: st[name][0]) if st is not None else (lambda name: None)
    x2 = x.reshape(m, d)

    proj = norm_matmul(x2, p['norm_mix'][0], p['hg_w_in']).reshape(b, t, -1)
    o, hg_s = hgrn_scan(proj, p['hg_lb'], p['hg_norm'], get('hgrn'), layer=0)
    x2 = mix_out_ffn(o.reshape(m, d), p['hg_w_out'], x2, p['norm_ffn'][0], p['ffn_w_in'][0], p['ffn_w_out'][0],
                     p['norm_final'], final_norm=False)

    proj = norm_matmul(x2, p['norm_mix'][1], p['ret_w_in'], out_dtype=BF16).reshape(b, t, -1)
    cos, sin = rope_table(pos0, t)
    o, ret_s = ret_scan(proj, cos, sin, p['ret_norm'], get('ret'))
    x2 = mix_out_ffn(o.reshape(m, -1), p['ret_w_out'], x2, p['norm_ffn'][1], p['ffn_w_in'][1], p['ffn_w_out'][1],
                     p['norm_final'], final_norm=False)

    s0 = get('rwkv')
    streams, shift = rwkv_proj(x2.reshape(b, t, d), p['norm_mix'][2], get('shift'), p['rw'])
    o, rw_s = rwkv_scan(streams, p['rw']['r_k'], p['rw']['ln_w'], p['rw']['ln_b'],
                        None if s0 is None else _rwkv_state_to_bd(s0))
    x2 = mix_out_ffn(o.reshape(m, d), p['rw']['w_out'], x2, p['norm_ffn'][2], p['ffn_w_in'][2], p['ffn_w_out'][2],
                     p['norm_final'], final_norm=False)

    proj = norm_matmul(x2, p['norm_mix'][3], p['lru_w_in']).reshape(b, t, -1)
    o, lru_s = lru_scan(proj, p['lru_conv_w'], p['lru_conv_b'], p['lru_w_gates'], p['lru_b_gates'],
                        p['lru_lambda'], get('lru'), get('conv'))
    if st is None:
        conv_s = proj[:, t - (CONV_W - 1):, d:]
    else:
        conv_s = jnp.concatenate([st['conv'][0], proj[:, :, d:]], axis=1)[:, -(CONV_W - 1):]
    y = mix_out_ffn(o.reshape(m, d), p['lru_w_out'], x2, p['norm_ffn'][3], p['ffn_w_in'][3], p['ffn_w_out'][3],
                    p['norm_final'], final_norm=True)
    return (y.reshape(b, t, d), hg_s[None], ret_s[None], rw_s[None], shift[None], lru_s[None], conv_s[None])


def kernel(x_prompt, x_sample, state_hgrn, state_ret, state_rwkv, state_rwkv_shift, state_lru, state_lru_conv, norm_mix, norm_ffn, norm_final, hg_lb, hg_w_in, hg_norm, hg_w_out, ret_w_in, ret_norm, ret_w_out, rw_mu, rw_w_rkv, rw_w0, rw_w1, rw_w2, rw_a0, rw_a1, rw_a2, rw_g1, rw_g2, rw_k_k, rw_k_a, rw_r_k, rw_ln_w, rw_ln_b, rw_w_out, lru_w_in, lru_conv_w, lru_conv_b, lru_w_gates, lru_b_gates, lru_lambda, lru_w_out, ffn_w_in, ffn_w_out):
    bf = lambda a: a.astype(BF16)
    p = dict(
        norm_mix=norm_mix, norm_ffn=norm_ffn, norm_final=norm_final,
        hg_lb=hg_lb, hg_w_in=bf(hg_w_in[0]), hg_norm=hg_norm[0], hg_w_out=bf(hg_w_out[0]),
        ret_w_in=bf(ret_w_in[0]), ret_norm=ret_norm[0], ret_w_out=bf(ret_w_out[0]),
        rw=dict(mu=rw_mu[0], w_rkv=bf(rw_w_rkv[0]), w0=rw_w0[0], w1=bf(rw_w1[0]), w2=bf(rw_w2[0]),
                a0=rw_a0[0], a1=bf(rw_a1[0]), a2=bf(rw_a2[0]), g1=bf(rw_g1[0]), g2=bf(rw_g2[0]),
                k_k=rw_k_k[0], k_a=rw_k_a[0], r_k=rw_r_k[0], ln_w=rw_ln_w[0], ln_b=rw_ln_b[0],
                w_out=bf(rw_w_out[0])),
        lru_w_in=bf(lru_w_in[0]), lru_conv_w=lru_conv_w[0], lru_conv_b=lru_conv_b[0],
        lru_w_gates=bf(lru_w_gates[0]), lru_b_gates=lru_b_gates[0], lru_lambda=lru_lambda[0],
        lru_w_out=bf(lru_w_out[0]),
        ffn_w_in=bf(ffn_w_in), ffn_w_out=bf(ffn_w_out),
    )
    yp, hg_p, ret_p, rw_p, sh_p, lru_p, conv_p = _trunk(x_prompt, 0, None, p)
    st = dict(hgrn=state_hgrn, ret=state_ret, rwkv=state_rwkv, shift=state_rwkv_shift,
              lru=state_lru, conv=state_lru_conv)
    ys, hg_s, ret_s, rw_s, sh_s, lru_s, conv_s = _trunk(x_sample, PAST_LEN, st, p)
    return (yp, ys, hg_p, hg_s, ret_p, ret_s, rw_p, rw_s, sh_p, sh_s, lru_p, lru_s, conv_p, conv_s)
```

```python
import functools
import math

import jax
import jax.numpy as jnp
from jax import lax
from jax.experimental import pallas as pl
from jax.experimental.pallas import tpu as pltpu

F32 = jnp.float32
BF16 = jnp.bfloat16

D_MODEL = 1024
NORM_EPS = 1e-6
HG_HEADS, HG_DK, HG_BLOCK = 8, 128, 16
HG_SUPER = 4
RET_HEADS, RET_DK, RET_DV = 4, 256, 512
ROPE_BASE = 10000.0
RW_HEADS, RW_HEAD = 16, 64
RW_GROUP = 4
RW_GN_EPS = 64e-5
LRU_BLOCKS, LRU_BW, CONV_W, LRU_C = 8, 128, 4, 8.0
D_FF = 2816
PAST_LEN = 2048

VMEM_LIMIT_BYTES = 56 * 1024 * 1024


def _cparams(*sem):
    return pltpu.CompilerParams(dimension_semantics=sem, vmem_limit_bytes=VMEM_LIMIT_BYTES)


def _const_spec(shape):
    n = len(shape)
    return pl.BlockSpec(tuple(shape), lambda *_: (0,) * n, pipeline_mode=pl.Buffered(1))


def _iota(shape, dim):
    return lax.broadcasted_iota(jnp.int32, shape, dim)


def _dot(a, b):
    return jnp.dot(a, b, preferred_element_type=F32)


def _dot_nt(a, b):
    return lax.dot_general(a, b, (((1,), (1,)), ((), ())), preferred_element_type=F32)


def _dot_tn(a, b):
    return lax.dot_general(a, b, (((0,), (0,)), ((), ())), preferred_element_type=F32)


def _split3(x):
    hi = x.astype(BF16)
    r1 = x - hi.astype(F32)
    mid = r1.astype(BF16)
    lo = (r1 - mid.astype(F32)).astype(BF16)
    return hi, mid, lo


def _sel_dot(m01, x, pieces=3):
    return sum(_dot(m01, p) for p in _split3(x)[:pieces])


def _dot_sel(x, m01, pieces=3):
    return sum(_dot(p, m01) for p in _split3(x)[:pieces])


def _rms(x, w):
    return x * lax.rsqrt(jnp.mean(x * x, axis=-1, keepdims=True) + NORM_EPS) * w


def _sigmoid(x):
    return 0.5 * jnp.tanh(0.5 * x) + 0.5


def _softplus(x):
    return jnp.maximum(x, 0.0) + jnp.log1p(jnp.exp(-jnp.abs(x)))


def _log2(n):
    l = int(math.log2(n))
    assert 1 << l == n, n
    return l


def _norm_mm_body(x_ref, nw_ref, w_ref, o_ref, *, tn):
    h = _rms(x_ref[...], nw_ref[...]).astype(BF16)
    for c in range(w_ref.shape[1] // tn):
        o_ref[:, c * tn:(c + 1) * tn] = _dot(h, w_ref[:, c * tn:(c + 1) * tn]).astype(o_ref.dtype)


def norm_matmul(x2d, nw, w, *, out_dtype=F32, tm=512, tn=1024):
    m, k = x2d.shape
    n = w.shape[1]
    tm, tn = min(tm, m), min(tn, n)
    return pl.pallas_call(
        functools.partial(_norm_mm_body, tn=tn),
        grid=(m // tm,),
        in_specs=[pl.BlockSpec((tm, k), lambda i: (i, 0)),
                  _const_spec((1, k)),
                  _const_spec((k, n))],
        out_specs=pl.BlockSpec((tm, n), lambda i: (i, 0)),
        out_shape=jax.ShapeDtypeStruct((m, n), out_dtype),
        compiler_params=_cparams("parallel"),
        name="norm_matmul",
    )(x2d, nw.reshape(1, k), w)


def _mix_ffn_body(o_ref, wo_ref, x_ref, nw_ref, win_ref, wout_ref, fnw_ref, y_ref, *, final_norm):
    x1 = x_ref[...] + _dot(o_ref[...].astype(BF16), wo_ref[...])
    h = _rms(x1, nw_ref[...]).astype(BF16)
    g = _dot(h, win_ref[:, 0:D_FF])
    u = _dot(h, win_ref[:, D_FF:2 * D_FF])
    act = (g * _sigmoid(g) * u).astype(BF16)
    acc = x1 + _dot(act, wout_ref[...])
    if final_norm:
        acc = _rms(acc, fnw_ref[...])
    y_ref[...] = acc


def mix_out_ffn(o2d, w_o, x2d, nw, w_in, w_out, fnw, *, final_norm, tm=512):
    m, ko = o2d.shape
    d = x2d.shape[1]
    tm = min(tm, m)
    return pl.pallas_call(
        functools.partial(_mix_ffn_body, final_norm=final_norm),
        grid=(m // tm,),
        in_specs=[pl.BlockSpec((tm, ko), lambda i: (i, 0)),
                  _const_spec((ko, d)),
                  pl.BlockSpec((tm, d), lambda i: (i, 0)),
                  _const_spec((1, d)),
                  _const_spec((d, 2 * D_FF)),
                  _const_spec((D_FF, d)),
                  _const_spec((1, d))],
        out_specs=pl.BlockSpec((tm, d), lambda i: (i, 0)),
        out_shape=jax.ShapeDtypeStruct((m, d), F32),
        compiler_params=_cparams("parallel"),
        name="mix_out_ffn",
    )(o2d, w_o, x2d, nw.reshape(1, d), w_in, w_out, fnw.reshape(1, d))


def _hgrn_body(*refs, tc, layer, has_state):
    if has_state:
        q_ref, f_ref, v_ref, g_ref, lbp_ref, gn_ref, s0_ref, o_ref, sout_ref, st_ref = refs
    else:
        q_ref, f_ref, v_ref, g_ref, lbp_ref, gn_ref, o_ref, sout_ref, st_ref = refs
    t = pl.program_id(1)
    blk = min(HG_BLOCK, tc)
    heads, dk = HG_HEADS, HG_DK
    nb = tc // blk
    m = min(HG_SUPER, nb)
    sbr = m * blk

    nrows = q_ref.shape[0]

    @pl.when(t == 0)
    def _():
        for rb in range(nrows):
            for h in range(heads):
                st_ref[rb, h] = s0_ref[rb, h].T if has_state else jnp.zeros((dk, dk), F32)

    lbp = lbp_ref[...]
    e = jnp.exp(lbp - jnp.max(lbp, axis=0, keepdims=True))
    lb = jnp.sum(e[:layer + 1], axis=0, keepdims=True) / jnp.sum(e, axis=0, keepdims=True)

    sh = _log2(blk)
    ri, ci = _iota((tc, tc), 0), _iota((tc, tc), 1)
    bdiff = (ri >> sh) - (ci >> sh)
    same_sb = (ri >> _log2(sbr)) == (ci >> _log2(sbr))
    causal = (bdiff == 0) & (ci <= ri)
    causal16, same16 = causal.astype(BF16), (bdiff == 0).astype(BF16)
    masks = [causal] + [same_sb & (bdiff == dd) for dd in range(1, m)]
    rows_of = lambda b: slice(b * blk, (b + 1) * blk)
    zero_blk = jnp.zeros((blk, heads * dk), F32)

    def prep(rb):
        half_th = 0.5 * jnp.tanh(0.5 * f_ref[rb])
        f = lb + (1.0 - lb) * (0.5 + half_th)
        k = (1.0 - lb) * (0.5 - half_th)
        lf = jnp.log(f)
        cum = _sel_dot(causal16, lf, pieces=2)
        tot = _sel_dot(same16, lf, pieces=2)
        q_in = q_ref[rb] * jnp.exp(cum)
        k_in = k * jnp.exp(-cum)
        kst1 = k * jnp.exp(tot - cum)
        drow = [jnp.exp(tot[b * blk:b * blk + 1, :]) for b in range(nb)]
        kst = {1: [kst1[rows_of(b)] for b in range(nb)]}
        for dd in range(2, m + 1):
            kst[dd] = [kst[dd - 1][b] * drow[b + dd - 1] if (b % m) + dd - 1 <= m - 1 else None for b in range(nb)]
        k_far = [jnp.concatenate([x if x is not None else zero_blk for x in kst[dd]], axis=0) for dd in range(2, m)]
        k_end = jnp.concatenate([kst[m - (b % m)][b] for b in range(nb)], axis=0)
        q_sb = []
        for b in range(nb):
            x = q_in[rows_of(b)]
            for l in range(1, (b % m) + 1):
                x = x * drow[b - l]
            q_sb.append(x)
        dec_sb = []
        for sbi in range(nb // m):
            x = drow[sbi * m]
            for b in range(sbi * m + 1, (sbi + 1) * m):
                x = x * drow[b]
            dec_sb.append(x)
        return dict(q_in=q_in, k_all=[k_in, kst1] + k_far, k_end=k_end, q_sb=jnp.concatenate(q_sb, axis=0),
                    dec_sb=dec_sb, v=v_ref[rb])

    rowd = [prep(rb) for rb in range(nrows)]
    units = [(rb, h) for rb in range(nrows) for h in range(heads)]
    sls = [slice(h * dk, (h + 1) * dk) for _, h in units]
    col = lambda name: [rowd[rb][name][:, sl] for (rb, _), sl in zip(units, sls)]
    q_in, k_end, q_sb, v = col('q_in'), col('k_end'), col('q_sb'), col('v')
    rhs = [jnp.concatenate([x[:, sl] for x in rowd[rb]['k_all']], axis=0).astype(BF16) for (rb, _), sl in zip(units, sls)]
    sc = [_dot_nt(q_in[u].astype(BF16), rhs[u]) for u in range(len(units))]
    sc = [sum(jnp.where(masks[dd], x[:, dd * tc:(dd + 1) * tc], 0.0) for dd in range(m)).astype(BF16) for x in sc]
    o = [_dot(sc[u], v[u].astype(BF16)) for u in range(len(units))]
    st = [st_ref[rb, h] for rb, h in units]
    o_in = [[] for _ in units]
    for sbi in range(nb // m):
        rows = slice(sbi * sbr, (sbi + 1) * sbr)
        for u in range(len(units)):
            o_in[u].append(_dot_nt(q_sb[u][rows].astype(BF16), st[u].astype(BF16)))
        kv = [_dot_tn(v[u][rows].astype(BF16), k_end[u][rows].astype(BF16)) for u in range(len(units))]
        st = [st[u] * rowd[rb]['dec_sb'][sbi][:, sls[u]] + kv[u] for u, (rb, _) in enumerate(units)]
    for u, (rb, h) in enumerate(units):
        st_ref[rb, h] = st[u]

    gn = gn_ref[...]
    for u, (rb, h) in enumerate(units):
        sl = sls[u]
        oh = o[u] + jnp.concatenate(o_in[u], axis=0)
        oh = oh * lax.rsqrt(jnp.mean(oh * oh, axis=-1, keepdims=True) + NORM_EPS)
        gh = g_ref[rb, :, sl]
        o_ref[rb, :, sl] = (oh * gn[:, sl] * (gh * _sigmoid(gh))).astype(BF16)

    @pl.when(t == pl.num_programs(1) - 1)
    def _():
        for rb in range(nrows):
            for h in range(heads):
                sout_ref[rb, h] = st_ref[rb, h].T


def hgrn_scan(proj, hg_lb, g_norm, s0, *, layer, tc=128, rows=2):
    b, t, _ = proj.shape
    d = D_MODEL
    tc = min(tc, t)
    nrows = rows if b % rows == 0 else 1
    has_state = s0 is not None
    col = lambda c: pl.BlockSpec((nrows, tc, d), lambda i, j, c=c: (i, j, c))
    in_specs = [col(0), col(1), col(2), col(3),
                pl.BlockSpec(hg_lb.shape, lambda i, j: (0, 0)),
                pl.BlockSpec((1, d), lambda i, j: (0, 0))]
    args = [proj, proj, proj, proj, hg_lb, g_norm.reshape(1, d)]
    st_spec = pl.BlockSpec((nrows, HG_HEADS, HG_DK, HG_DK), lambda i, j: (i, 0, 0, 0))
    if has_state:
        in_specs.append(st_spec)
        args.append(s0)
    return pl.pallas_call(
        functools.partial(_hgrn_body, tc=tc, layer=layer, has_state=has_state),
        grid=(b // nrows, t // tc),
        in_specs=in_specs,
        out_specs=[pl.BlockSpec((nrows, tc, d), lambda i, j: (i, j, 0)), st_spec],
        out_shape=[jax.ShapeDtypeStruct((b, t, d), BF16),
                   jax.ShapeDtypeStruct((b, HG_HEADS, HG_DK, HG_DK), F32)],
        scratch_shapes=[pltpu.VMEM((nrows, HG_HEADS, HG_DK, HG_DK), F32)],
        compiler_params=_cparams("parallel", "arbitrary"),
        name="hgrn_scan",
    )(*args)


def _rope_body(cos_ref, sin_ref, *, pos0, tc):
    half = RET_DK // 2
    pos = (pos0 + pl.program_id(0) * tc + _iota((tc, half), 0)).astype(F32)
    inv = jnp.power(jnp.float32(ROPE_BASE), -(_iota((tc, half), 1).astype(F32) / half))
    ang = pos * inv
    cos_ref[...] = jnp.cos(ang)
    sin_ref[...] = jnp.sin(ang)


def rope_table(pos0, t, *, tc=256):
    tc = min(tc, t)
    half = RET_DK // 2
    return pl.pallas_call(
        functools.partial(_rope_body, pos0=pos0, tc=tc),
        grid=(t // tc,),
        out_specs=[pl.BlockSpec((tc, half), lambda i: (i, 0))] * 2,
        out_shape=[jax.ShapeDtypeStruct((t, half), F32)] * 2,
        compiler_params=_cparams("parallel"),
        name="rope_table",
    )()


def _ret_body(*refs, tc, has_state):
    if has_state:
        q_ref, k_ref, v_ref, g_ref, cos_ref, sin_ref, gn_ref, s0_ref, o_ref, sout_ref, s_ref, dmat_ref = refs
    else:
        q_ref, k_ref, v_ref, g_ref, cos_ref, sin_ref, gn_ref, o_ref, sout_ref, s_ref, dmat_ref = refs
    t = pl.program_id(1)
    nh, dk, dv = RET_HEADS, RET_DK, RET_DV
    half = dk // 2
    heads = range(nh)
    lg = [math.log1p(-2.0 ** (-5.0 - h)) for h in heads]

    @pl.when(t == 0)
    def _():
        s_ref[...] = s0_ref[0] if has_state else jnp.zeros((nh, dk, dv), F32)
        diff = (_iota((tc, tc), 0) - _iota((tc, tc), 1)).astype(F32)
        for h in heads:
            dmat_ref[h] = jnp.where(diff >= 0.0, jnp.exp(diff * lg[h]), 0.0)

    cos, sin = cos_ref[...], sin_ref[...]

    def rot(x):
        x1, x2 = x[:, :half], x[:, half:]
        return jnp.concatenate([x1 * cos - x2 * sin, x1 * sin + x2 * cos], axis=-1)

    ksl = [slice(h * dk, (h + 1) * dk) for h in heads]
    vsl = [slice(h * dv, (h + 1) * dv) for h in heads]
    row = _iota((tc, 1), 0).astype(F32)
    q = [rot(q_ref[0, :, sl].astype(F32)) for sl in ksl]
    k = [rot(k_ref[0, :, sl].astype(F32)) * (dk ** -0.5) for sl in ksl]
    v = [v_ref[0, :, sl].astype(BF16) for sl in vsl]
    s = [s_ref[h] for h in heads]
    sc = [_dot_nt(q[h].astype(BF16), k[h].astype(BF16)) for h in heads]
    sc = [(dmat_ref[h] * sc[h]).astype(BF16) for h in heads]
    o = [_dot(sc[h], v[h]) + _dot((q[h] * jnp.exp((row + 1.0) * lg[h])).astype(BF16), s[h].astype(BF16))
         for h in heads]
    kv = [_dot_tn((k[h] * jnp.exp((tc - 1.0 - row) * lg[h])).astype(BF16), v[h]) for h in heads]
    for h in heads:
        s_ref[h] = s[h] * math.exp(tc * lg[h]) + kv[h]
        oh = o[h] * lax.rsqrt(jnp.mean(o[h] * o[h], axis=-1, keepdims=True) + NORM_EPS)
        g = g_ref[0, :, vsl[h]].astype(F32)
        o_ref[0, :, vsl[h]] = (oh * gn_ref[:, vsl[h]] * (g * _sigmoid(g))).astype(BF16)

    @pl.when(t == pl.num_programs(1) - 1)
    def _():
        sout_ref[0] = s_ref[...]


def ret_scan(proj, cos, sin, g_norm, s0, *, tc=256):
    b, t, _ = proj.shape
    tc = min(tc, t)
    has_state = s0 is not None
    nh, dk, dv = RET_HEADS, RET_DK, RET_DV
    half = dk // 2
    in_specs = [pl.BlockSpec((1, tc, nh * dk), lambda i, j: (i, j, 0)),
                pl.BlockSpec((1, tc, nh * dk), lambda i, j: (i, j, 1)),
                pl.BlockSpec((1, tc, nh * dv), lambda i, j: (i, j, 1)),
                pl.BlockSpec((1, tc, nh * dv), lambda i, j: (i, j, 2)),
                pl.BlockSpec((tc, half), lambda i, j: (j, 0)),
                pl.BlockSpec((tc, half), lambda i, j: (j, 0)),
                _const_spec((1, nh * dv))]
    args = [proj, proj, proj, proj, cos, sin, g_norm.reshape(1, nh * dv)]
    st_spec = pl.BlockSpec((1, nh, dk, dv), lambda i, j: (i, 0, 0, 0))
    if has_state:
        in_specs.append(st_spec)
        args.append(s0)
    return pl.pallas_call(
        functools.partial(_ret_body, tc=tc, has_state=has_state),
        grid=(b, t // tc),
        in_specs=in_specs,
        out_specs=[pl.BlockSpec((1, tc, nh * dv), lambda i, j: (i, j, 0)), st_spec],
        out_shape=[jax.ShapeDtypeStruct((b, t, nh * dv), BF16),
                   jax.ShapeDtypeStruct((b, nh, dk, dv), F32)],
        scratch_shapes=[pltpu.VMEM((nh, dk, dv), F32), pltpu.VMEM((nh, tc, tc), F32)],
        compiler_params=_cparams("parallel", "arbitrary"),
        name="ret_scan",
    )(*args)


def _rwkv_proj_body(*refs, has_state):
    if has_state:
        (x_ref, nw_ref, sh0_ref, mu_ref, wrkv_ref, w0_ref, w1_ref, w2_ref, a0_ref, a1_ref, a2_ref,
         g1_ref, g2_ref, kk_ref, ka_ref, hsum_ref, hexp_ref,
         r_o, k_o, v_o, kk_o, a_o, lw_o, g_o, sh_o, carry_ref) = refs
    else:
        (x_ref, nw_ref, mu_ref, wrkv_ref, w0_ref, w1_ref, w2_ref, a0_ref, a1_ref, a2_ref,
         g1_ref, g2_ref, kk_ref, ka_ref, hsum_ref, hexp_ref,
         r_o, k_o, v_o, kk_o, a_o, lw_o, g_o, sh_o, carry_ref) = refs
    t = pl.program_id(1)
    tm = x_ref.shape[1]

    @pl.when(t == 0)
    def _():
        carry_ref[...] = sh0_ref[0] if has_state else jnp.zeros((1, D_MODEL), F32)

    h = _rms(x_ref[0], nw_ref[...])
    prev = jnp.where(_iota((tm, 1), 0) == 0, carry_ref[...], pltpu.roll(h, 1, axis=0))
    last = h[tm - 1:tm, :]
    carry_ref[...] = last
    sh_o[0] = last
    d = prev - h
    mix = lambda i: (h + d * mu_ref[i:i + 1, :]).astype(BF16)
    r = _dot(mix(0), wrkv_ref[0])
    k = _dot(mix(1), wrkv_ref[1])
    v = _dot(mix(2), wrkv_ref[2])
    w_pre = w0_ref[...] + _dot(jnp.tanh(_dot(mix(3), w1_ref[...])).astype(BF16), w2_ref[...])
    lw_o[0] = -math.exp(-0.5) * _sigmoid(w_pre)
    a = _sigmoid(a0_ref[...] + _dot(_dot(mix(4), a1_ref[...]).astype(BF16), a2_ref[...]))
    g_o[0] = _dot(_sigmoid(_dot(mix(5), g1_ref[...])).astype(BF16), g2_ref[...]).astype(g_o.dtype)
    kk = k * kk_ref[...]
    ss = _dot_sel(kk * kk, hsum_ref[...], pieces=2)
    inv = 1.0 / jnp.maximum(jnp.sqrt(ss), 1e-12)
    kk_o[0] = (kk * _dot_sel(inv, hexp_ref[...], pieces=2)).astype(kk_o.dtype)
    r_o[0] = r.astype(r_o.dtype)
    k_o[0] = (k * (1.0 + (a - 1.0) * ka_ref[...])).astype(k_o.dtype)
    v_o[0] = v.astype(v_o.dtype)
    a_o[0] = a


def rwkv_proj(x, nw, shift0, p, *, tm=256):
    b, t, d = x.shape
    tm = min(tm, t)
    has_state = shift0 is not None
    row = lambda a: a.reshape(1, d)
    head_of_lane = jnp.arange(d) // RW_HEAD
    hsum = (head_of_lane[:, None] == jnp.arange(RW_HEADS)[None, :]).astype(BF16)
    hexp = hsum.T
    full = lambda a: pl.BlockSpec(a.shape, lambda i, j, n=a.ndim: (0,) * n)
    tok = pl.BlockSpec((1, tm, d), lambda i, j: (i, j, 0))
    args = [x, row(nw)]
    in_specs = [tok, full(row(nw))]
    if has_state:
        args.append(shift0.reshape(b, 1, d))
        in_specs.append(pl.BlockSpec((1, 1, d), lambda i, j: (i, 0, 0)))
    consts = [p['mu'], p['w_rkv'], row(p['w0']), p['w1'], p['w2'], row(p['a0']), p['a1'], p['a2'],
              p['g1'], p['g2'], row(p['k_k']), row(p['k_a']), hsum, hexp]
    args += consts
    in_specs += [full(a) for a in consts]
    outs = pl.pallas_call(
        functools.partial(_rwkv_proj_body, has_state=has_state),
        grid=(b, t // tm),
        in_specs=in_specs,
        out_specs=[tok] * 7 + [pl.BlockSpec((1, 1, d), lambda i, j: (i, 0, 0))],
        out_shape=[jax.ShapeDtypeStruct((b, t, d), dt) for dt in (BF16, BF16, BF16, BF16, F32, F32, BF16)]
        + [jax.ShapeDtypeStruct((b, 1, d), F32)],
        scratch_shapes=[pltpu.VMEM((1, d), F32)],
        compiler_params=_cparams("parallel", "arbitrary"),
        name="rwkv_proj",
    )(*args)
    return outs[:7], outs[7].reshape(b, d)


def _rwkv_scan_body(*refs, L, has_state):
    if has_state:
        (r_ref, k_ref, v_ref, kk_ref, a_ref, lw_ref, g_ref, rk_ref, lnw_ref, lnb_ref, s0_ref,
         o_ref, sout_ref, s_ref) = refs
    else:
        (r_ref, k_ref, v_ref, kk_ref, a_ref, lw_ref, g_ref, rk_ref, lnw_ref, lnb_ref,
         o_ref, sout_ref, s_ref) = refs
    t = pl.program_id(1)
    gw = RW_GROUP * RW_HEAD
    ngroups = RW_HEADS // RW_GROUP
    sl_l, sl_h = _log2(L), _log2(RW_HEAD)
    n = RW_GROUP * L

    @pl.when(t == 0)
    def _():
        s_ref[...] = s0_ref[...] if has_state else jnp.zeros(s_ref.shape, F32)

    nrows = lw_ref.shape[0]
    tri = (_iota((L, L), 1) <= _iota((L, L), 0)).astype(BF16)
    v_r, a_hat_r, r_hat_r, b_hat_r, k_hat_r, b_til_r, k_til_r, gam_l_r, bonus_r = ([] for _ in range(9))
    for rb in range(nrows):
        lw = lw_ref[rb]
        cum = _sel_dot(tri, lw)
        cum_l = cum[L - 1:L, :]
        e_pos, e_neg = jnp.exp(cum), jnp.exp(-cum)
        e_tail = jnp.exp(cum_l - cum)
        kk, a, r, k = kk_ref[rb].astype(F32), a_ref[rb], r_ref[rb].astype(F32), k_ref[rb].astype(F32)
        beta = kk * a
        v_r.append(v_ref[rb].astype(F32))
        a_hat_r.append(-kk * jnp.exp(cum - lw))
        r_hat_r.append(r * e_pos)
        b_hat_r.append(beta * e_neg)
        k_hat_r.append(k * e_neg)
        b_til_r.append(beta * e_tail)
        k_til_r.append(k * e_tail)
        gam_l_r.append(jnp.exp(cum_l))
        bonus_r.append(r * k * rk_ref[...])

    own = (_iota((n, 1), 0) >> sl_l) == (_iota((1, gw), 1) >> sl_h)
    tile = lambda x: jnp.concatenate([x] * RW_GROUP, axis=0)
    stack = lambda x: jnp.where(own, tile(x), 0.0).astype(BF16)
    same = (_iota((n, n), 0) >> sl_l) == (_iota((n, n), 1) >> sl_l)
    to_bd = lambda x: jnp.where(same, tile(x), 0.0).astype(BF16)
    row_c, col_c = _iota((L, n), 0), _iota((L, n), 1) & (L - 1)
    strict_c, incl_c = col_c < row_c, col_c <= row_c
    eye_c = (col_c == row_c).astype(F32)
    bdg = (_iota((gw, gw), 0) >> sl_h) == (_iota((gw, gw), 1) >> sl_h)
    bdg16 = bdg.astype(BF16)
    inv_n = 1.0 / RW_HEAD

    units = [(rb, gi) for rb in range(nrows) for gi in range(ngroups)]
    groups = range(len(units))
    sls = [slice(gi * gw, (gi + 1) * gw) for _, gi in units]
    pick = lambda per_row: [per_row[rb][:, sl] for (rb, _), sl in zip(units, sls)]
    v_u, b_til_u, k_til_u, bonus_u = pick(v_r), pick(b_til_r), pick(k_til_r), pick(bonus_r)
    gam_l_u = pick(gam_l_r)
    ar = [jnp.concatenate([x, y], axis=0).astype(BF16) for x, y in zip(pick(a_hat_r), pick(r_hat_r))]
    c = [_dot_nt(ar[u], jnp.concatenate([stack(x), stack(y)], axis=0))
         for u, (x, y) in enumerate(zip(pick(b_hat_r), pick(k_hat_r)))]
    p_c = [jnp.where(strict_c, x[:L, :n], 0.0) for x in c]
    m_k = [jnp.concatenate([jnp.where(strict_c, x[:L, n:], 0.0), jnp.where(incl_c, x[L:, n:], 0.0)],
                           axis=0).astype(BF16) for x in c]
    m_rb = [jnp.where(incl_c, x[L:, :n], 0.0).astype(BF16) for x in c]
    t_c = [eye_c + x for x in p_c]
    for j in range(sl_l):
        p_bd = [to_bd(x) for x in p_c]
        if j == 0:
            p_c = [_dot(p_c[gi].astype(BF16), p_bd[gi]) for gi in groups]
        elif j < sl_l - 1:
            both = [_dot(jnp.concatenate([p_c[gi], t_c[gi]], axis=0).astype(BF16), p_bd[gi]) for gi in groups]
            p_c = [x[:L] for x in both]
            t_c = [t_c[gi] + both[gi][L:] for gi in groups]
        else:
            t_c = [t_c[gi] + _dot(t_c[gi].astype(BF16), p_bd[gi]) for gi in groups]
    s_g = [s_ref[rb, gi] for rb, gi in units]
    q = [_dot_nt(ar[u], s_g[u].astype(BF16)) for u in groups]
    kv = [_dot(m_k[u], stack(v_u[u])) for u in groups]
    u_c = [_dot(t_c[u].astype(BF16), stack(q[u][:L] + kv[u][:L])) for u in groups]
    y = [q[u][L:] + kv[u][L:] + _dot(m_rb[u], stack(u_c[u])) for u in groups]
    for u, (rb, gi) in enumerate(units):
        upd = _dot_tn(jnp.concatenate([u_c[u], v_u[u]], axis=0).astype(BF16),
                      jnp.concatenate([b_til_u[u], k_til_u[u]], axis=0).astype(BF16))
        s_ref[rb, gi] = s_g[u] * gam_l_u[u] + jnp.where(bdg, upd, 0.0)
    pieces = []
    for u in groups:
        pieces += _split3(y[u]) + _split3(y[u] * y[u]) + _split3(bonus_u[u])
    sums = _dot(jnp.concatenate(pieces, axis=0), bdg16)
    for u, (rb, gi) in enumerate(units):
        sl = sls[u]
        sum3 = lambda i, base=9 * u * L: sums[base + 3 * i * L:base + (3 * i + 1) * L] \
            + sums[base + (3 * i + 1) * L:base + (3 * i + 2) * L] + sums[base + (3 * i + 2) * L:base + (3 * i + 3) * L]
        mu = sum3(0) * inv_n
        var = sum3(1) * inv_n - mu * mu
        yn = (y[u] - mu) * lax.rsqrt(var + RW_GN_EPS) * lnw_ref[:, sl] + lnb_ref[:, sl]
        o_ref[rb, :, sl] = ((yn + sum3(2) * v_u[u]) * g_ref[rb, :, sl].astype(F32)).astype(BF16)

    @pl.when(t == pl.num_programs(1) - 1)
    def _():
        hd = RW_HEAD
        for rb in range(nrows):
            for gi in range(ngroups):
                s_bd = s_ref[rb, gi]
                for i in range(RW_GROUP):
                    sout_ref[rb, RW_GROUP * gi + i] = s_bd[i * hd:(i + 1) * hd, i * hd:(i + 1) * hd]


def rwkv_scan(streams, r_k, ln_w, ln_b, s0_bd, *, chunk=64, rows=4):
    b, t, d = streams[0].shape
    L = min(chunk, t)
    nrows = rows if b % rows == 0 else 1
    has_state = s0_bd is not None
    ngroups = RW_HEADS // RW_GROUP
    gw = RW_GROUP * RW_HEAD
    tok = pl.BlockSpec((nrows, L, d), lambda i, j: (i, j, 0))
    vec = pl.BlockSpec((1, d), lambda i, j: (0, 0))
    st_spec = pl.BlockSpec((nrows, ngroups, gw, gw), lambda i, j: (i, 0, 0, 0))
    args = list(streams) + [r_k.reshape(1, d), ln_w.reshape(1, d), ln_b.reshape(1, d)]
    in_specs = [tok] * 7 + [vec] * 3
    if has_state:
        args.append(s0_bd)
        in_specs.append(st_spec)
    return pl.pallas_call(
        functools.partial(_rwkv_scan_body, L=L, has_state=has_state),
        grid=(b // nrows, t // L),
        in_specs=in_specs,
        out_specs=[tok, pl.BlockSpec((nrows, RW_HEADS, RW_HEAD, RW_HEAD), lambda i, j: (i, 0, 0, 0))],
        out_shape=[jax.ShapeDtypeStruct((b, t, d), BF16),
                   jax.ShapeDtypeStruct((b, RW_HEADS, RW_HEAD, RW_HEAD), F32)],
        scratch_shapes=[pltpu.VMEM((nrows, ngroups, gw, gw), F32)],
        compiler_params=_cparams("parallel", "arbitrary"),
        name="rwkv_scan",
    )(*args)


def _rwkv_state_to_bd(s):
    b = s.shape[0]
    g, m, hd = RW_HEADS // RW_GROUP, RW_GROUP, RW_HEAD
    s = s.reshape(b, g, m, hd, 1, hd) * jnp.eye(m, dtype=s.dtype)[None, None, :, None, :, None]
    return s.reshape(b, g, m * hd, m * hd)


def _lru_body(*refs, tc, has_state):
    if has_state:
        (gate_ref, xb_ref, cw_ref, cb_ref, wg_ref, bg_ref, lam_ref, h0_ref, c0_ref,
         y_ref, hout_ref, xpad_ref, hc_ref) = refs
    else:
        (gate_ref, xb_ref, cw_ref, cb_ref, wg_ref, bg_ref, lam_ref,
         y_ref, hout_ref, xpad_ref, hc_ref) = refs
    t = pl.program_id(1)
    d = D_MODEL
    pad = 8

    @pl.when(t == 0)
    def _():
        xpad_ref[0:pad, :] = c0_ref[0] if has_state else jnp.zeros((pad, d), F32)
        hc_ref[...] = h0_ref[0] if has_state else jnp.zeros((1, d), F32)

    xpad_ref[pad:pad + tc, :] = xb_ref[0]
    xc = cb_ref[...]
    for j in range(CONV_W):
        off = pad - (CONV_W - 1) + j
        xc = xc + xpad_ref[off:off + tc, :] * cw_ref[j:j + 1, :]
    xpad_ref[0:pad, :] = xpad_ref[tc:tc + pad, :]

    xcb = xc.astype(BF16)
    r_pre, i_pre = [], []
    for nb in range(LRU_BLOCKS):
        sl = slice(nb * LRU_BW, (nb + 1) * LRU_BW)
        r_pre.append(_dot(xcb[:, sl], wg_ref[0, nb]))
        i_pre.append(_dot(xcb[:, sl], wg_ref[1, nb]))
    r_gate = _sigmoid(jnp.concatenate(r_pre, axis=-1) + bg_ref[0:1, :])
    i_gate = _sigmoid(jnp.concatenate(i_pre, axis=-1) + bg_ref[1:2, :])
    log_a = -LRU_C * r_gate * _softplus(-lam_ref[...])
    a = jnp.exp(log_a)
    om = -jnp.tanh(log_a) * (a * a + 1.0)
    bv = jnp.where(om > 0.0, om * lax.rsqrt(om), 0.0) * (i_gate * xc)

    grp = 8
    a = a.reshape(tc // grp, grp, d)
    bv = bv.reshape(tc // grp, grp, d)
    sub = _iota((1, grp, 1), 1)
    s = 1
    while s < grp:
        keep = sub >= s
        a_sh = jnp.where(keep, pltpu.roll(a, s, axis=1), 1.0)
        b_sh = jnp.where(keep, pltpu.roll(bv, s, axis=1), 0.0)
        bv = bv + a * b_sh
        a = a * a_sh
        s *= 2
    carry = hc_ref[...]
    rows = []
    for gi in range(tc // grp):
        h_g = bv[gi] + a[gi] * carry
        carry = h_g[grp - 1:grp, :]
        rows.append(h_g)
    hs = jnp.concatenate(rows, axis=0)
    last = carry
    hc_ref[...] = last
    hout_ref[0] = last
    y_ref[0] = (hs * jax.nn.gelu(gate_ref[0], approximate=True)).astype(BF16)


def lru_scan(proj, conv_w, conv_b, w_gates, b_gates, lam, h0, conv0, *, tc=256):
    b, t, _ = proj.shape
    d = D_MODEL
    tc = min(tc, t)
    has_state = h0 is not None
    full = lambda a: pl.BlockSpec(a.shape, lambda i, j, n=a.ndim: (0,) * n)
    consts = [conv_w, conv_b.reshape(1, d), w_gates, b_gates, lam.reshape(1, d)]
    args = [proj, proj] + consts
    in_specs = [pl.BlockSpec((1, tc, d), lambda i, j: (i, j, 0)),
                pl.BlockSpec((1, tc, d), lambda i, j: (i, j, 1))] + [full(a) for a in consts]
    if has_state:
        c0 = jnp.pad(conv0, ((0, 0), (8 - (CONV_W - 1), 0), (0, 0)))
        args += [h0.reshape(b, 1, d), c0]
        in_specs += [pl.BlockSpec((1, 1, d), lambda i, j: (i, 0, 0)),
                     pl.BlockSpec((1, 8, d), lambda i, j: (i, 0, 0))]
    y, hl = pl.pallas_call(
        functools.partial(_lru_body, tc=tc, has_state=has_state),
        grid=(b, t // tc),
        in_specs=in_specs,
        out_specs=[pl.BlockSpec((1, tc, d), lambda i, j: (i, j, 0)),
                   pl.BlockSpec((1, 1, d), lambda i, j: (i, 0, 0))],
        out_shape=[jax.ShapeDtypeStruct((b, t, d), BF16), jax.ShapeDtypeStruct((b, 1, d), F32)],
        scratch_shapes=[pltpu.VMEM((tc + 8, d), F32), pltpu.VMEM((1, d), F32)],
        compiler_params=_cparams("parallel", "arbitrary"),
        name="lru_scan",
    )(*args)
    return y, hl.reshape(b, d)


def _trunk(x, pos0, st, p):
    b, t, d = x.shape
    m = b * t
    get = (lambda name: st[name][0]) if st is not None else (lambda name: None)
    x2 = x.reshape(m, d)

    proj = norm_matmul(x2, p['norm_mix'][0], p['hg_w_in']).reshape(b, t, -1)
    o, hg_s = hgrn_scan(proj, p['hg_lb'], p['hg_norm'], get('hgrn'), layer=0)
    x2 = mix_out_ffn(o.reshape(m, d), p['hg_w_out'], x2, p['norm_ffn'][0], p['ffn_w_in'][0], p['ffn_w_out'][0],
                     p['norm_final'], final_norm=False)

    proj = norm_matmul(x2, p['norm_mix'][1], p['ret_w_in'], out_dtype=BF16).reshape(b, t, -1)
    cos, sin = rope_table(pos0, t)
    o, ret_s = ret_scan(proj, cos, sin, p['ret_norm'], get('ret'))
    x2 = mix_out_ffn(o.reshape(m, -1), p['ret_w_out'], x2, p['norm_ffn'][1], p['ffn_w_in'][1], p['ffn_w_out'][1],
                     p['norm_final'], final_norm=False)

    s0 = get('rwkv')
    streams, shift = rwkv_proj(x2.reshape(b, t, d), p['norm_mix'][2], get('shift'), p['rw'])
    o, rw_s = rwkv_scan(streams, p['rw']['r_k'], p['rw']['ln_w'], p['rw']['ln_b'],
                        None if s0 is None else _rwkv_state_to_bd(s0))
    x2 = mix_out_ffn(o.reshape(m, d), p['rw']['w_out'], x2, p['norm_ffn'][2], p['ffn_w_in'][2], p['ffn_w_out'][2],
                     p['norm_final'], final_norm=False)

    proj = norm_matmul(x2, p['norm_mix'][3], p['lru_w_in']).reshape(b, t, -1)
    o, lru_s = lru_scan(proj, p['lru_conv_w'], p['lru_conv_b'], p['lru_w_gates'], p['lru_b_gates'],
                        p['lru_lambda'], get('lru'), get('conv'))
    if st is None:
        conv_s = proj[:, t - (CONV_W - 1):, d:]
    else:
        conv_s = jnp.concatenate([st['conv'][0], proj[:, :, d:]], axis=1)[:, -(CONV_W - 1):]
    y = mix_out_ffn(o.reshape(m, d), p['lru_w_out'], x2, p['norm_ffn'][3], p['ffn_w_in'][3], p['ffn_w_out'][3],
                    p['norm_final'], final_norm=True)
    return (y.reshape(b, t, d), hg_s[None], ret_s[None], rw_s[None], shift[None], lru_s[None], conv_s[None])


def kernel(x_prompt, x_sample, state_hgrn, state_ret, state_rwkv, state_rwkv_shift, state_lru, state_lru_conv, norm_mix, norm_ffn, norm_final, hg_lb, hg_w_in, hg_norm, hg_w_out, ret_w_in, ret_norm, ret_w_out, rw_mu, rw_w_rkv, rw_w0, rw_w1, rw_w2, rw_a0, rw_a1, rw_a2, rw_g1, rw_g2, rw_k_k, rw_k_a, rw_r_k, rw_ln_w, rw_ln_b, rw_w_out, lru_w_in, lru_conv_w, lru_conv_b, lru_w_gates, lru_b_gates, lru_lambda, lru_w_out, ffn_w_in, ffn_w_out):
    bf = lambda a: a.astype(BF16)
    p = dict(
        norm_mix=norm_mix, norm_ffn=norm_ffn, norm_final=norm_final,
        hg_lb=hg_lb, hg_w_in=bf(hg_w_in[0]), hg_norm=hg_norm[0], hg_w_out=bf(hg_w_out[0]),
        ret_w_in=bf(ret_w_in[0]), ret_norm=ret_norm[0], ret_w_out=bf(ret_w_out[0]),
        rw=dict(mu=rw_mu[0], w_rkv=bf(rw_w_rkv[0]), w0=rw_w0[0], w1=bf(rw_w1[0]), w2=bf(rw_w2[0]),
                a0=rw_a0[0], a1=bf(rw_a1[0]), a2=bf(rw_a2[0]), g1=bf(rw_g1[0]), g2=bf(rw_g2[0]),
                k_k=rw_k_k[0], k_a=rw_k_a[0], r_k=rw_r_k[0], ln_w=rw_ln_w[0], ln_b=rw_ln_b[0],
                w_out=bf(rw_w_out[0])),
        lru_w_in=bf(lru_w_in[0]), lru_conv_w=lru_conv_w[0], lru_conv_b=lru_conv_b[0],
        lru_w_gates=bf(lru_w_gates[0]), lru_b_gates=lru_b_gates[0], lru_lambda=lru_lambda[0],
        lru_w_out=bf(lru_w_out[0]),
        ffn_w_in=bf(ffn_w_in), ffn_w_out=bf(ffn_w_out),
    )
    yp, hg_p, ret_p, rw_p, sh_p, lru_p, conv_p = _trunk(x_prompt, 0, None, p)
    st = dict(hgrn=state_hgrn, ret=state_ret, rwkv=state_rwkv, shift=state_rwkv_shift,
              lru=state_lru, conv=state_lru_conv)
    ys, hg_s, ret_s, rw_s, sh_s, lru_s, conv_s = _trunk(x_sample, PAST_LEN, st, p)
    return (yp, ys, hg_p, hg_s, ret_p, ret_s, rw_p, rw_s, sh_p, sh_s, lru_p, lru_s, conv_p, conv_s)
```
